```python
import math, functools
import jax, jax.numpy as jnp
from jax import lax
import numpy as np

D_MODEL = 2048
BATCH = 4
SEQ = 2048
DEPTH = 4
DEC_BATCH = 128
DEC_SEQ = 1
PAST_LEN = 8192
PAGE_SIZE = 128

F32 = jnp.float32
EPS = 1e-6
ROPE_THETA = 10000.0
Q_BLOCK = 128
NEG_INF = -1e30
N_MEM = 256
D_FF = 5632

GROUP_WIDTH = D_MODEL // 4
D_MIX = 4 * GROUP_WIDTH

MLA_HEADS = 8
MLA_NOPE = GROUP_WIDTH // MLA_HEADS
MLA_ROPE = MLA_NOPE // 2
MLA_VD = GROUP_WIDTH // MLA_HEADS
MLA_Q_RANK = (3 * D_MODEL) // 16
MLA_KV_RANK = D_MODEL // 16
FOX_HEADS = 8
FOX_HD = GROUP_WIDTH // FOX_HEADS
SB_HEADS = 8
SB_HD = GROUP_WIDTH // SB_HEADS
DIFF_HEADS = 8
DIFF_VD = GROUP_WIDTH // DIFF_HEADS
DIFF_D = DIFF_VD // 2
MEM_HEADS = 4
MEM_HD = 128

MLA_SCALE = (MLA_NOPE + MLA_ROPE) ** -0.5
FOX_SCALE = FOX_HD ** -0.5
SB_SCALE = SB_HD ** -0.5
DIFF_SCALE = DIFF_D ** -0.5
MEM_SCALE = MEM_HD ** -0.5

MLA_ROW = MLA_KV_RANK + MLA_ROPE
FOX_ROW = 2 * FOX_HD + FOX_HEADS
SB_ROW = 2 * SB_HD
DIFF_ROW = 2 * DIFF_D + DIFF_VD

IN_SIZES = (MLA_Q_RANK, MLA_KV_RANK, MLA_ROPE,
            FOX_HEADS * FOX_HD, FOX_HD, FOX_HD, FOX_HEADS,
            SB_HEADS * SB_HD, SB_HD, SB_HD,
            DIFF_HEADS * 2 * DIFF_D, 2 * DIFF_D, DIFF_VD)
IN_COLS = sum(IN_SIZES)

kernel_name = "hybrid_mla_fox_stickbreak_diff_decoder_step"


def rmsnorm(x, g):
    xf = x.astype(F32)
    y = xf * lax.rsqrt(jnp.mean(xf * xf, axis=-1, keepdims=True) + EPS) * g.astype(F32)
    return y.astype(x.dtype)


def rope(x, pos):
    d = x.shape[-1]
    half = d // 2
    inv = ROPE_THETA ** (-jnp.arange(half, dtype=F32) * (2.0 / d))
    ang = pos.astype(F32)[:, None] * inv[None, :]
    shape = (1, pos.shape[0]) + (1,) * (x.ndim - 3) + (half,)
    cos = jnp.cos(ang).reshape(shape)
    sin = jnp.sin(ang).reshape(shape)
    xf = x.astype(F32)
    x1, x2 = xf[..., :half], xf[..., half:]
    return jnp.concatenate([x1 * cos - x2 * sin, x2 * cos + x1 * sin], axis=-1).astype(x.dtype)


def split_cols(u, sizes):
    out, start = [], 0
    for s in sizes:
        out.append(u[..., start:start + s])
        start += s
    return out


def swiglu(x, g, w1, w3, w2):
    h = rmsnorm(x, g)
    return (jax.nn.silu(h @ w1) * (h @ w3)) @ w2


def mixer_project(h, pos, P):
    B, T, _ = h.shape
    (cq, ckv, kpe, fq, fk, fv, fz, sq, sk, sv, dq, dk, dv) = split_cols(h @ P["w_in"], IN_SIZES)
    cq = rmsnorm(cq, P["mla_cq_norm"])
    q = (cq @ P["mla_w_uq"]).reshape(B, T, MLA_HEADS, MLA_NOPE + MLA_ROPE)
    q_nope = rmsnorm(q[..., :MLA_NOPE], P["mla_qn_gain"])
    q_pe = rope(rmsnorm(q[..., MLA_NOPE:], P["mla_qr_gain"]), pos)
    ckv = rmsnorm(ckv, P["mla_ckv_norm"])
    kpe = rope(rmsnorm(kpe, P["mla_kr_gain"]), pos)
    fq = rmsnorm(fq.reshape(B, T, FOX_HEADS, FOX_HD), P["fox_q_gain"])
    fk = rmsnorm(fk, P["fox_k_gain"])
    logf = jax.nn.log_sigmoid(fz.astype(F32) + P["fox_f_bias"].astype(F32)).astype(h.dtype)
    sq = sq.reshape(B, T, SB_HEADS, SB_HD)
    dq = rope(rmsnorm(dq.reshape(B, T, DIFF_HEADS, 2, DIFF_D), P["diff_q_gain"]), pos)
    dk = rope(rmsnorm(dk.reshape(B, T, 2, DIFF_D), P["diff_k_gain"]), pos)
    rows = (jnp.concatenate([ckv, kpe], axis=-1),
            jnp.concatenate([fk, fv, logf], axis=-1),
            jnp.concatenate([sk, sv], axis=-1),
            jnp.concatenate([dk.reshape(B, T, 2 * DIFF_D), dv], axis=-1))
    return (q_nope, q_pe, fq, sq, dq), rows


def key_side(rows, P):
    mla_row, fox_row, sb_row, diff_row = rows
    B, Tk, _ = mla_row.shape
    ckv, kpe = mla_row[..., :MLA_KV_RANK], mla_row[..., MLA_KV_RANK:]
    kn = jnp.einsum('bsc,chd->bshd', ckv, P["mla_w_uk"])
    kn_inv = lax.rsqrt(jnp.einsum('bshd,bshd->bhs', kn, kn, preferred_element_type=F32) / MLA_NOPE + EPS)
    fk, fv, logf = split_cols(fox_row, (FOX_HD, FOX_HD, FOX_HEADS))
    fc = jnp.transpose(jnp.cumsum(logf.astype(F32), axis=1), (0, 2, 1))
    sk, sv = split_cols(sb_row, (SB_HD, SB_HD))
    dk = diff_row[..., :2 * DIFF_D].reshape(B, Tk, 2, DIFF_D)
    dv = diff_row[..., 2 * DIFF_D:]
    return (ckv, kpe, kn, kn_inv, fk, fv, fc, sk, sv, dk, dv)


def attend_block(qs, q_pos, ks, k_pos, P, lam, lam_init, dtype):
    q_nope, q_pe, fq, sq, dq = qs
    ckv, kpe, kn, kn_inv, fk, fv, fc, sk, sv, dk, dv = ks
    B, Tq = q_nope.shape[:2]
    causal = k_pos[None, :] <= q_pos[:, None]
    strict = k_pos[None, :] < q_pos[:, None]
    s = jnp.einsum('bqhd,bshd->bhqs', q_nope * P["mla_kn_gain"], kn, preferred_element_type=F32) * kn_inv[:, :, None, :]
    s = (s + jnp.einsum('bqhd,bsd->bhqs', q_pe, kpe, preferred_element_type=F32)) * MLA_SCALE
    p = jax.nn.softmax(jnp.where(causal, s, NEG_INF), axis=-1)
    o_lat = jnp.einsum('bhqs,bsc->bqhc', p.astype(dtype), ckv, preferred_element_type=F32)
    o_a = jnp.einsum('bqhc,chd->bqhd', o_lat, P["mla_w_uv"].astype(F32))
    fcq = jnp.take(fc, q_pos, axis=2)
    bias = fcq[..., None] - fc[:, :, None, :]
    s = jnp.einsum('bqhd,bsd->bhqs', fq, fk, preferred_element_type=F32) * FOX_SCALE + bias
    p = jax.nn.softmax(jnp.where(causal, s, NEG_INF), axis=-1)
    o_b = jnp.einsum('bhqs,bsd->bqhd', p.astype(dtype), fv, preferred_element_type=F32)
    z = jnp.einsum('bqhd,bsd->bhqs', sq, sk, preferred_element_type=F32) * SB_SCALE
    lsn = jnp.where(strict, jax.nn.log_sigmoid(-z), 0.0)
    log_w = jax.nn.log_sigmoid(z) + lax.cumsum(lsn, axis=3, reverse=True) - lsn
    w = jnp.where(strict, jnp.exp(log_w), 0.0)
    o_c = jnp.einsum('bhqs,bsd->bqhd', w.astype(dtype), sv, preferred_element_type=F32)
    z = jnp.einsum('bqhcd,bscd->bchqs', dq, dk, preferred_element_type=F32) * DIFF_SCALE
    p = jax.nn.softmax(jnp.where(causal, z, NEG_INF), axis=-1)
    a = p[:, 0] - lam * p[:, 1]
    o_d = jnp.einsum('bhqs,bsd->bqhd', a.astype(dtype), dv, preferred_element_type=F32)
    o_d = rmsnorm(o_d, P["diff_sub_gain"]) * (1.0 - lam_init)
    out = jnp.concatenate([o.reshape(B, Tq, -1) for o in (o_a, o_b, o_c, o_d)], axis=-1)
    return out.astype(dtype)


def prompt_mix(h, P, lam, lam_init):
    B, S, _ = h.shape
    pos = jnp.arange(S, dtype=jnp.int32)
    qs, rows = mixer_project(h, pos, P)
    ks = key_side(rows, P)

    def block(i):
        start = i * Q_BLOCK
        qsb = tuple(lax.dynamic_slice_in_dim(a, start, Q_BLOCK, axis=1) for a in qs)
        q_pos = start + jnp.arange(Q_BLOCK, dtype=jnp.int32)
        return attend_block(qsb, q_pos, ks, pos, P, lam, lam_init, h.dtype)

    o = lax.map(block, jnp.arange(S // Q_BLOCK, dtype=jnp.int32))
    return jnp.transpose(o, (1, 0, 2, 3)).reshape(B, S, D_MIX), rows


def sample_mix(h, past_rows, P, lam, lam_init):
    T = h.shape[1]
    past_len = past_rows[0].shape[1]
    pos = past_len + jnp.arange(T, dtype=jnp.int32)
    qs, rows = mixer_project(h, pos, P)
    full = tuple(jnp.concatenate([pr, r], axis=1) for pr, r in zip(past_rows, rows))
    ks = key_side(full, P)
    k_pos = jnp.arange(past_len + T, dtype=jnp.int32)
    return attend_block(qs, pos, ks, k_pos, P, lam, lam_init, h.dtype), rows


def memory_kv(mem, P):
    B, M, _ = mem.shape
    mn = rmsnorm(mem, P["mem_kv_norm"])
    k = rmsnorm((mn @ P["mem_w_k"]).reshape(B, M, MEM_HEADS, MEM_HD), P["mem_k_gain"])
    v = (mn @ P["mem_w_v"]).reshape(B, M, MEM_HEADS, MEM_HD)
    return jnp.stack([k, v], axis=2)


def memory_attend(x, kv, P):
    B, T, _ = x.shape
    q = rmsnorm((rmsnorm(x, P["mem_q_norm"]) @ P["mem_w_q"]).reshape(B, T, MEM_HEADS, MEM_HD), P["mem_q_gain"])
    s = jnp.einsum('bqhd,bmhd->bhqm', q, kv[:, :, 0], preferred_element_type=F32) * MEM_SCALE
    p = jax.nn.softmax(s, axis=-1).astype(x.dtype)
    o = jnp.einsum('bhqm,bmhd->bqhd', p, kv[:, :, 1])
    return o.reshape(B, T, MEM_HEADS * MEM_HD) @ P["mem_w_o"]


def run_layer(x, mix_fn, mem_kv, P):
    x = x + 0.5 * swiglu(x, P["ffn1_norm"], P["ffn1_w1"], P["ffn1_w3"], P["ffn1_w2"])
    o, rows = mix_fn(rmsnorm(x, P["mix_norm"]))
    x = x + o @ P["w_out"]
    x = x + memory_attend(x, mem_kv, P)
    x = x + 0.5 * swiglu(x, P["ffn2_norm"], P["ffn2_w1"], P["ffn2_w3"], P["ffn2_w2"])
    return x, rows


def gather_pages(pool, l, page_table):
    g = pool[l, page_table]
    return g.reshape(g.shape[0], g.shape[1] * g.shape[2], g.shape[3])


def setup_inputs(seed: int = 0) -> dict:
    key = jax.random.key(seed)
    kit = iter(jax.random.split(key, 80))

    def nrm(shape, scale):
        return jax.random.normal(next(kit), shape, F32) * scale

    def gain(shape):
        return 1.0 + 0.05 * jax.random.normal(next(kit), shape, F32)

    n_pages = PAST_LEN // PAGE_SIZE
    n_used = DEC_BATCH * n_pages
    n_pool = n_used + max(1, n_used // 4)
    L = DEPTH
    x_prompt = nrm((BATCH, SEQ, D_MODEL), 1.0)
    x_sample = nrm((DEC_BATCH, DEC_SEQ, D_MODEL), 1.0)
    mem_prompt = nrm((BATCH, N_MEM, D_MODEL), 1.0)
    cache_mla = nrm((L, n_pool, PAGE_SIZE, MLA_ROW), 1.0)
    cache_fox = jnp.concatenate([
        nrm((L, n_pool, PAGE_SIZE, 2 * FOX_HD), 1.0),
        jax.nn.log_sigmoid(2.0 + nrm((L, n_pool, PAGE_SIZE, FOX_HEADS), 1.0))], axis=-1)
    cache_sb = nrm((L, n_pool, PAGE_SIZE, SB_ROW), 1.0)
    cache_diff = nrm((L, n_pool, PAGE_SIZE, DIFF_ROW), 1.0)
    cache_mem = nrm((L, DEC_BATCH, N_MEM, 2, MEM_HEADS, MEM_HD), 1.0)
    page_table = jax.random.permutation(next(kit), n_pool)[:n_used].reshape(DEC_BATCH, n_pages).astype(jnp.int32)
    return {
        "x_prompt": x_prompt, "x_sample": x_sample, "mem_prompt": mem_prompt,
        "cache_mla": cache_mla, "cache_fox": cache_fox, "cache_sb": cache_sb,
        "cache_diff": cache_diff, "cache_mem": cache_mem, "page_table": page_table,
        "ffn1_norm": gain((L, D_MODEL)),
        "ffn1_w1": nrm((L, D_MODEL, D_FF), D_MODEL ** -0.5),
        "ffn1_w3": nrm((L, D_MODEL, D_FF), D_MODEL ** -0.5),
        "ffn1_w2": nrm((L, D_FF, D_MODEL), D_FF ** -0.5),
        "mix_norm": gain((L, D_MODEL)),
        "w_in": nrm((L, D_MODEL, IN_COLS), D_MODEL ** -0.5),
        "fox_f_bias": 2.0 + nrm((L, FOX_HEADS), 0.1),
        "mla_cq_norm": gain((L, MLA_Q_RANK)),
        "mla_ckv_norm": gain((L, MLA_KV_RANK)),
        "mla_w_uq": nrm((L, MLA_Q_RANK, MLA_HEADS * (MLA_NOPE + MLA_ROPE)), MLA_Q_RANK ** -0.5),
        "mla_w_uk": nrm((L, MLA_KV_RANK, MLA_HEADS, MLA_NOPE), MLA_KV_RANK ** -0.5),
        "mla_w_uv": nrm((L, MLA_KV_RANK, MLA_HEADS, MLA_VD), MLA_KV_RANK ** -0.5),
        "mla_qn_gain": gain((L, MLA_NOPE)),
        "mla_qr_gain": gain((L, MLA_ROPE)),
        "mla_kn_gain": gain((L, MLA_NOPE)),
        "mla_kr_gain": gain((L, MLA_ROPE)),
        "fox_q_gain": gain((L, FOX_HD)),
        "fox_k_gain": gain((L, FOX_HD)),
        "diff_q_gain": gain((L, DIFF_D)),
        "diff_k_gain": gain((L, DIFF_D)),
        "diff_lam_q1": nrm((L, DIFF_D), 0.1),
        "diff_lam_k1": nrm((L, DIFF_D), 0.1),
        "diff_lam_q2": nrm((L, DIFF_D), 0.1),
        "diff_lam_k2": nrm((L, DIFF_D), 0.1),
        "diff_sub_gain": gain((L, DIFF_VD)),
        "w_out": nrm((L, D_MIX, D_MODEL), D_MIX ** -0.5),
        "mem_q_norm": gain((L, D_MODEL)),
        "mem_kv_norm": gain((L, D_MODEL)),
        "mem_w_q": nrm((L, D_MODEL, MEM_HEADS * MEM_HD), D_MODEL ** -0.5),
        "mem_w_k": nrm((L, D_MODEL, MEM_HEADS * MEM_HD), D_MODEL ** -0.5),
        "mem_w_v": nrm((L, D_MODEL, MEM_HEADS * MEM_HD), D_MODEL ** -0.5),
        "mem_q_gain": gain((L, MEM_HD)),
        "mem_k_gain": gain((L, MEM_HD)),
        "mem_w_o": nrm((L, MEM_HEADS * MEM_HD, D_MODEL), (MEM_HEADS * MEM_HD) ** -0.5),
        "ffn2_norm": gain((L, D_MODEL)),
        "ffn2_w1": nrm((L, D_MODEL, D_FF), D_MODEL ** -0.5),
        "ffn2_w3": nrm((L, D_MODEL, D_FF), D_MODEL ** -0.5),
        "ffn2_w2": nrm((L, D_FF, D_MODEL), D_FF ** -0.5),
    }


def reference(x_prompt, x_sample, mem_prompt, cache_mla, cache_fox, cache_sb, cache_diff, cache_mem, page_table,
              ffn1_norm, ffn1_w1, ffn1_w3, ffn1_w2, mix_norm, w_in, fox_f_bias,
              mla_cq_norm, mla_ckv_norm, mla_w_uq, mla_w_uk, mla_w_uv,
              mla_qn_gain, mla_qr_gain, mla_kn_gain, mla_kr_gain,
              fox_q_gain, fox_k_gain,
              diff_q_gain, diff_k_gain, diff_lam_q1, diff_lam_k1, diff_lam_q2, diff_lam_k2, diff_sub_gain,
              w_out, mem_q_norm, mem_kv_norm, mem_w_q, mem_w_k, mem_w_v, mem_q_gain, mem_k_gain, mem_w_o,
              ffn2_norm, ffn2_w1, ffn2_w3, ffn2_w2):
    xp, xs = x_prompt, x_sample
    p_mla, p_fox, p_sb, p_diff, p_mem = [], [], [], [], []
    s_mla, s_fox, s_sb, s_diff = [], [], [], []
    for l in range(DEPTH):
        P = {
            "ffn1_norm": ffn1_norm[l], "ffn1_w1": ffn1_w1[l], "ffn1_w3": ffn1_w3[l], "ffn1_w2": ffn1_w2[l],
            "mix_norm": mix_norm[l], "w_in": w_in[l], "fox_f_bias": fox_f_bias[l],
            "mla_cq_norm": mla_cq_norm[l], "mla_ckv_norm": mla_ckv_norm[l],
            "mla_w_uq": mla_w_uq[l], "mla_w_uk": mla_w_uk[l], "mla_w_uv": mla_w_uv[l],
            "mla_qn_gain": mla_qn_gain[l], "mla_qr_gain": mla_qr_gain[l],
            "mla_kn_gain": mla_kn_gain[l], "mla_kr_gain": mla_kr_gain[l],
            "fox_q_gain": fox_q_gain[l], "fox_k_gain": fox_k_gain[l],
            "diff_q_gain": diff_q_gain[l], "diff_k_gain": diff_k_gain[l], "diff_sub_gain": diff_sub_gain[l],
            "w_out": w_out[l],
            "mem_q_norm": mem_q_norm[l], "mem_kv_norm": mem_kv_norm[l],
            "mem_w_q": mem_w_q[l], "mem_w_k": mem_w_k[l], "mem_w_v": mem_w_v[l],
            "mem_q_gain": mem_q_gain[l], "mem_k_gain": mem_k_gain[l], "mem_w_o": mem_w_o[l],
            "ffn2_norm": ffn2_norm[l], "ffn2_w1": ffn2_w1[l], "ffn2_w3": ffn2_w3[l], "ffn2_w2": ffn2_w2[l],
        }
        lam_init = 0.8 - 0.6 * math.exp(-0.3 * l)
        lam = (jnp.exp(jnp.sum(diff_lam_q1[l].astype(F32) * diff_lam_k1[l].astype(F32)))
               - jnp.exp(jnp.sum(diff_lam_q2[l].astype(F32) * diff_lam_k2[l].astype(F32))) + lam_init)
        mkv = memory_kv(mem_prompt, P)
        xp, rows_p = run_layer(xp, functools.partial(prompt_mix, P=P, lam=lam, lam_init=lam_init), mkv, P)
        past = tuple(gather_pages(c, l, page_table) for c in (cache_mla, cache_fox, cache_sb, cache_diff))
        xs, rows_s = run_layer(xs, functools.partial(sample_mix, past_rows=past, P=P, lam=lam, lam_init=lam_init),
                               cache_mem[l], P)
        p_mla.append(rows_p[0]); p_fox.append(rows_p[1]); p_sb.append(rows_p[2]); p_diff.append(rows_p[3])
        p_mem.append(mkv)
        s_mla.append(rows_s[0]); s_fox.append(rows_s[1]); s_sb.append(rows_s[2]); s_diff.append(rows_s[3])
    return (xp, xs,
            jnp.stack(p_mla), jnp.stack(p_fox), jnp.stack(p_sb), jnp.stack(p_diff), jnp.stack(p_mem),
            jnp.stack(s_mla), jnp.stack(s_fox), jnp.stack(s_sb), jnp.stack(s_diff))
```

```python
import functools
import math

import numpy as np
import jax
import jax.numpy as jnp
from jax import lax
from jax.experimental import pallas as pl
from jax.experimental.pallas import tpu as pltpu

F32 = jnp.float32
BF16 = jnp.bfloat16
EPS = 1e-6
ROPE_THETA = 10000.0
NEG_INF = -1e30

LANES = 128
VMEM_LIMIT_BYTES = 56 * 1024 * 1024

HEADS = 8
HD = 64
ROPE_D = 32
MLA_Q_RANK = 384
MLA_KV_RANK = 128
MEM_HEADS = 4
MEM_HD = 128
GROUP = HEADS * HD

MLA_SCALE = (HD + ROPE_D) ** -0.5
FOX_SCALE = HD ** -0.5
SB_SCALE = HD ** -0.5
DIFF_SCALE = ROPE_D ** -0.5
MEM_SCALE = MEM_HD ** -0.5

C_CQ, C_CKV, C_KPE, C_FQ, C_FKV, C_FZ, C_SQ, C_SKV, C_DQ, C_DKV, C_END = (
    0, 384, 512, 640, 1152, 1280, 1408, 1920, 2048, 2560, 2688)
IN_SIZES = (384, 128, 32, 512, 64, 64, 8, 512, 64, 64, 512, 64, 64)

(G_CQ, G_CKV, G_Q1, G_Q2, G_KR, G_FQ, G_FK, G_FB, G_DQ, G_DK, G_QCNT) = range(11)
GV_ROWS = 16
GV_W = 1024


def _cparams(sem):
    return pltpu.CompilerParams(dimension_semantics=sem, vmem_limit_bytes=VMEM_LIMIT_BYTES)


def _dot(a, b):
    return jnp.dot(a, b, preferred_element_type=F32)


def _dot_nt(a, b):
    return lax.dot_general(a, b, (((1,), (1,)), ((), ())), preferred_element_type=F32)


def _rms_rows(x, g):
    return x * lax.rsqrt(jnp.mean(x * x, axis=-1, keepdims=True) + EPS) * g


def _split2(x):
    hi = x.astype(BF16)
    lo = (x - hi.astype(F32)).astype(BF16)
    return hi, lo


def _split3(x):
    h1 = x.astype(BF16)
    r = x - h1.astype(F32)
    h2 = r.astype(BF16)
    h3 = (r - h2.astype(F32)).astype(BF16)
    return h1, h2, h3


def _seg_inv_rms(x, e, et, inv_cnt):
    hi, lo = _split2(x * x)
    ss = _dot(hi, e) + _dot(lo, e)
    inv = lax.rsqrt(ss * inv_cnt + EPS)
    ihi, ilo = _split2(inv)
    return _dot(ihi, et) + _dot(ilo, et)


def _log_sigmoid(z):
    return jnp.minimum(z, 0.0) - jnp.log1p(jnp.exp(-jnp.abs(z)))


def _swap_halves(x):
    w = x.shape[1]
    lane = lax.broadcasted_iota(jnp.int32, x.shape, 1)
    return jnp.where((lane % ROPE_D) < (ROPE_D // 2), pltpu.roll(x, w - ROPE_D // 2, 1),
                     pltpu.roll(x, ROPE_D // 2, 1))


def _tile_lanes(t, n):
    return jnp.concatenate([t] * n, axis=1)


def _ffn_kernel(x_ref, g_ref, w1_ref, w3_ref, w2_ref, o_ref, h_ref, acc_ref):
    f = pl.program_id(1)

    @pl.when(f == 0)
    def _():
        h_ref[...] = _rms_rows(x_ref[...], g_ref[...]).astype(BF16)
        acc_ref[...] = jnp.zeros_like(acc_ref)

    h = h_ref[...]
    a = _dot(h, w1_ref[...])
    b = _dot(h, w3_ref[...])
    act = (a * jax.nn.sigmoid(a)) * b
    acc_ref[...] += _dot(act.astype(BF16), w2_ref[...])

    @pl.when(f == pl.num_programs(1) - 1)
    def _():
        o_ref[...] = x_ref[...] + 0.5 * acc_ref[...]


def _ffn(x, g, w1, w3, w2, tm, tf):
    m, d = x.shape
    dff = w1.shape[1]
    return pl.pallas_call(
        _ffn_kernel,
        out_shape=jax.ShapeDtypeStruct((m, d), F32),
        grid=(m // tm, dff // tf),
        in_specs=[
            pl.BlockSpec((tm, d), lambda i, f: (i, 0)),
            pl.BlockSpec((1, d), lambda i, f: (0, 0)),
            pl.BlockSpec((d, tf), lambda i, f: (0, f)),
            pl.BlockSpec((d, tf), lambda i, f: (0, f)),
            pl.BlockSpec((tf, d), lambda i, f: (f, 0)),
        ],
        out_specs=pl.BlockSpec((tm, d), lambda i, f: (i, 0)),
        scratch_shapes=[pltpu.VMEM((tm, d), BF16), pltpu.VMEM((tm, d), F32)],
        compiler_params=_cparams(("parallel", "arbitrary")),
        name="ffn",
    )(x, g, w1, w3, w2)


def _proj_kernel(x_ref, g_ref, win_ref, wuq_ref, wuk_ref, gv_ref, tab_ref,
                 eq_ref, etq_ref, ek_ref, etk_ref, e64_ref, et64_ref, e32_ref, et32_ref,
                 qmla_ref, qfsd_ref, kmla_ref, ksh_ref, mla_row_ref, fox_row_ref, sb_row_ref, diff_row_ref):
    h = _rms_rows(x_ref[...], g_ref[...]).astype(BF16)
    u = _dot(h, win_ref[...])
    tab = tab_ref[...]
    c_m, s_m = tab[:, 0:LANES], tab[:, LANES:2 * LANES]
    c_d, s_d = tab[:, 2 * LANES:3 * LANES], tab[:, 3 * LANES:4 * LANES]
    lane = lax.broadcasted_iota(jnp.int32, (u.shape[0], LANES), 1)
    lo64 = lane < HD

    def rope(x, c, s):
        n = x.shape[1] // LANES
        return x * _tile_lanes(c, n) + _swap_halves(x) * _tile_lanes(s, n)

    cqn = _rms_rows(u[:, C_CQ:C_CKV], gv_ref[G_CQ:G_CQ + 1, 0:MLA_Q_RANK]).astype(BF16)
    q = _dot(cqn, wuq_ref[...])
    inv = _seg_inv_rms(q, eq_ref[...], etq_ref[...], gv_ref[G_QCNT:G_QCNT + 1, 0:LANES])
    qn = q * inv * gv_ref[G_Q1:G_Q1 + 1, :] * gv_ref[G_Q2:G_Q2 + 1, :]
    qmla_ref[...] = (rope(qn, c_m, s_m) * MLA_SCALE).astype(BF16)

    ckvn = _rms_rows(u[:, C_CKV:C_KPE], gv_ref[G_CKV:G_CKV + 1, 0:LANES])
    kp = u[:, C_KPE:C_FQ]
    kpn = kp * lax.rsqrt(jnp.sum(kp * kp, axis=-1, keepdims=True) * (1.0 / ROPE_D) + EPS) \
        * gv_ref[G_KR:G_KR + 1, 0:LANES]
    kpr = rope(kpn, c_d, s_d)
    mla_row_ref[:, 0:LANES] = ckvn
    mla_row_ref[:, LANES:LANES + ROPE_D] = kpr[:, 0:ROPE_D]
    ckvb = ckvn.astype(BF16)
    kn = _dot(ckvb, wuk_ref[...])
    kns = kn * _seg_inv_rms(kn, ek_ref[...], etk_ref[...], 1.0 / HD)
    kmla_ref[...] = (kns + _tile_lanes(pltpu.roll(kpr, HD, 1), HEADS)).astype(BF16)

    fq = u[:, C_FQ:C_FKV]
    fqn = fq * _seg_inv_rms(fq, e64_ref[...], et64_ref[...], 1.0 / HD) * gv_ref[G_FQ:G_FQ + 1, 0:GROUP]
    fkv = u[:, C_FKV:C_FZ]
    ssf = jnp.sum(jnp.where(lo64, fkv * fkv, 0.0), axis=-1, keepdims=True)
    fkvn = jnp.where(lo64, fkv * lax.rsqrt(ssf * (1.0 / HD) + EPS) * gv_ref[G_FK:G_FK + 1, 0:LANES], fkv)
    logf = _log_sigmoid(u[:, C_FZ:C_SQ] + gv_ref[G_FB:G_FB + 1, 0:LANES])
    fox_row_ref[:, 0:LANES] = fkvn
    fox_row_ref[:, LANES:LANES + HEADS] = logf[:, 0:HEADS]

    skv = u[:, C_SKV:C_DQ]
    sb_row_ref[...] = skv

    dq = u[:, C_DQ:C_DKV]
    dqn = dq * _seg_inv_rms(dq, e32_ref[...], et32_ref[...], 1.0 / ROPE_D) * gv_ref[G_DQ:G_DQ + 1, 0:GROUP]
    dkv = u[:, C_DKV:C_END]
    d2 = dkv * dkv
    ss0 = jnp.sum(jnp.where(lane < ROPE_D, d2, 0.0), axis=-1, keepdims=True)
    ss1 = jnp.sum(jnp.where(jnp.logical_and(lane >= ROPE_D, lo64), d2, 0.0), axis=-1, keepdims=True)
    invd = jnp.where(lane < ROPE_D, lax.rsqrt(ss0 * (1.0 / ROPE_D) + EPS), lax.rsqrt(ss1 * (1.0 / ROPE_D) + EPS))
    dkr = rope(dkv * invd * gv_ref[G_DK:G_DK + 1, 0:LANES], c_d, s_d)
    dkvn = jnp.where(lo64, dkr, dkv)
    diff_row_ref[...] = dkvn

    qfsd_ref[:, 0:GROUP] = (fqn * FOX_SCALE).astype(BF16)
    qfsd_ref[:, GROUP:2 * GROUP] = (u[:, C_SQ:C_SKV] * SB_SCALE).astype(BF16)
    qfsd_ref[:, 2 * GROUP:3 * GROUP] = (rope(dqn, c_d, s_d) * DIFF_SCALE).astype(BF16)

    ksh_ref[:, 0:LANES] = ckvb
    for k, v in enumerate((fkvn, skv, dkvn)):
        ksh_ref[:, (1 + 2 * k) * LANES:(2 + 2 * k) * LANES] = v.astype(BF16)
        ksh_ref[:, (2 + 2 * k) * LANES:(3 + 2 * k) * LANES] = pltpu.roll(v, HD, 1).astype(BF16)


KSH_W = 7 * LANES


def _proj(x, g, win, wuq, wuk, gv, tab, emats, tm):
    m, d = x.shape
    const = lambda a: pl.BlockSpec(a.shape, lambda i: (0,) * a.ndim)
    row = lambda w: pl.BlockSpec((tm, w), lambda i: (i, 0))
    out_w = (8 * LANES, 3 * GROUP, 8 * LANES, KSH_W, 160, 136, 128, 128)
    out_dt = (BF16, BF16, BF16, BF16, F32, F32, F32, F32)
    return pl.pallas_call(
        _proj_kernel,
        out_shape=tuple(jax.ShapeDtypeStruct((m, w), t) for w, t in zip(out_w, out_dt)),
        grid=(m // tm,),
        in_specs=[row(d), const(g), const(win), const(wuq), const(wuk), const(gv), row(4 * LANES)]
                 + [const(e) for e in emats],
        out_specs=tuple(row(w) for w in out_w),
        compiler_params=_cparams(("parallel",)),
        name="mixer_proj",
    )(x, g, win, wuq, wuk, gv, tab, *emats)


def _fc_kernel(lc_ref, lr_ref, fcc_ref, fcr_ref, *, tb):
    s = lc_ref.shape[1]
    r = lax.broadcasted_iota(jnp.int32, (tb, tb), 0)
    c = lax.broadcasted_iota(jnp.int32, (tb, tb), 1)
    tril = (c <= r).astype(BF16)
    triu = (r <= c).astype(BF16)
    carry_c = jnp.zeros((1, LANES), F32)
    carry_r = jnp.zeros((HEADS, 1), F32)
    for j in range(s // tb):
        xc = lc_ref[0, j * tb:(j + 1) * tb, :]
        cs = sum(_dot(tril, p) for p in _split3(xc)) + carry_c
        fcc_ref[0, j * tb:(j + 1) * tb, :] = cs
        carry_c = cs[tb - 1:tb, :]
        xr = lr_ref[0, :, j * tb:(j + 1) * tb]
        csr = sum(_dot(p, triu) for p in _split3(xr)) + carry_r
        fcr_ref[0, j] = csr
        carry_r = csr[:, tb - 1:tb]


def _fc(logf_col, logf_row, tb):
    b, s, _ = logf_col.shape
    return pl.pallas_call(
        functools.partial(_fc_kernel, tb=tb),
        out_shape=(jax.ShapeDtypeStruct((b, s, LANES), F32), jax.ShapeDtypeStruct((b, s // tb, HEADS, tb), F32)),
        grid=(b,),
        in_specs=[pl.BlockSpec((1, s, LANES), lambda i: (i, 0, 0)), pl.BlockSpec((1, HEADS, s), lambda i: (i, 0, 0))],
        out_specs=(pl.BlockSpec((1, s, LANES), lambda i: (i, 0, 0)),
                   pl.BlockSpec((1, s // tb, HEADS, tb), lambda i: (i, 0, 0, 0))),
        compiler_params=_cparams(("parallel",)),
        name="forget_prefix",
    )(logf_col, logf_row)


def _diff_lambda(lam_ref, lam_init):
    a = jnp.sum(lam_ref[0:1, :] * lam_ref[1:2, :], axis=-1, keepdims=True)
    b = jnp.sum(lam_ref[2:3, :] * lam_ref[3:4, :], axis=-1, keepdims=True)
    return jnp.exp(a) - jnp.exp(b) + lam_init


def _attn_prompt_kernel(qm_ref, fq_ref, sq_ref, dq_ref, km_ref, ks_ref, fcc_ref, fcr_ref, wuv_ref, dg_ref, lam_ref,
                        oa_ref, ob_ref, oc_ref, od_ref, *, tq, tk, lam_init):
    p = pl.program_id(1)
    i = pl.program_id(2)
    lane = lax.broadcasted_iota(jnp.int32, (tq, LANES), 1)
    lo = lane < HD
    fqp, sqp, dqp = (r[...].astype(F32) for r in (fq_ref, sq_ref, dq_ref))
    q_mla = (qm_ref[:, 0:LANES], qm_ref[:, LANES:2 * LANES])
    q_fox = (jnp.where(lo, fqp, 0.0).astype(BF16), jnp.where(lo, 0.0, fqp).astype(BF16))
    q_sb = (jnp.where(lo, sqp, 0.0).astype(BF16), jnp.where(lo, 0.0, sqp).astype(BF16))
    q_dif = tuple(jnp.where(jnp.logical_and(lane >= ROPE_D * c, lane < ROPE_D * (c + 1)), dqp, 0.0).astype(BF16)
                  for c in range(4))

    fcc = fcc_ref[0]
    fcq = tuple(jnp.sum(jnp.where(lane == 2 * p + r, fcc, 0.0), axis=-1, keepdims=True) for r in range(2))

    rr = lax.broadcasted_iota(jnp.int32, (tk, tk), 0)
    cc = lax.broadcasted_iota(jnp.int32, (tk, tk), 1)
    t_after = (rr > cc).astype(BF16)

    def softmax_update(st, q, k, v, bias, mask):
        m, l, acc = st
        s = _dot_nt(q, k)
        if bias is not None:
            s = s + bias
        if mask is not None:
            s = jnp.where(mask, s, NEG_INF)
        m_new = jnp.maximum(m, jnp.max(s, axis=-1, keepdims=True))
        alpha = jnp.exp(m - m_new)
        e = jnp.exp(s - m_new)
        l = alpha * l + jnp.sum(e, axis=-1, keepdims=True)
        acc = alpha * acc + _dot(e.astype(BF16), v)
        return m_new, l, acc

    def sb_update(st, q, k, v, strict):
        r_sum, acc = st
        z = _dot_nt(q, k)
        lsp = _log_sigmoid(z)
        lsn = lsp - z
        if strict is not None:
            lsn = jnp.where(strict, lsn, 0.0)
        hi, lo2 = _split2(lsn)
        cs = _dot(hi, t_after) + _dot(lo2, t_after)
        w = jnp.exp(lsp + cs + r_sum)
        if strict is not None:
            w = jnp.where(strict, w, 0.0)
        acc = acc + _dot(w.astype(BF16), v)
        return r_sum + jnp.sum(lsn, axis=-1, keepdims=True), acc

    def step(j, carry, masked):
        mla, fox, sb, dif = carry
        ks0 = pl.multiple_of(j * tk, tk)
        kmb = km_ref[pl.ds(ks0, tk), :]
        ksb = ks_ref[pl.ds(ks0, tk), :]
        seg = lambda n: ksb[:, n * LANES:(n + 1) * LANES]
        ckv, kf, kfs, ksk, kss, kd, kds = (seg(n) for n in range(7))
        causal = strict = None
        if masked:
            row = lax.broadcasted_iota(jnp.int32, (tq, tk), 0)
            col = lax.broadcasted_iota(jnp.int32, (tq, tk), 1)
            causal, strict = col <= row, col < row
        mla = tuple(softmax_update(mla[r], q_mla[r], kmb[:, r * LANES:(r + 1) * LANES], ckv, None, causal)
                    for r in range(2))
        fck = tuple(fcr_ref[0, j, pl.ds(2 * p + r, 1), :] for r in range(2))
        fox = (softmax_update(fox[0], q_fox[0], kf, kfs, fcq[0] - fck[0], causal),
               softmax_update(fox[1], q_fox[1], kfs, kf, fcq[1] - fck[1], causal))
        sb = (sb_update(sb[0], q_sb[0], ksk, kss, strict), sb_update(sb[1], q_sb[1], kss, ksk, strict))
        dif = (softmax_update(dif[0], q_dif[0], kd, kds, None, causal),
               softmax_update(dif[1], q_dif[1], kd, kds, None, causal),
               softmax_update(dif[2], q_dif[2], kds, kd, None, causal),
               softmax_update(dif[3], q_dif[3], kds, kd, None, causal))
        return mla, fox, sb, dif

    def sm0():
        return (jnp.full((tq, 1), NEG_INF, F32), jnp.zeros((tq, 1), F32), jnp.zeros((tq, LANES), F32))

    carry = ((sm0(), sm0()), (sm0(), sm0()),
             ((jnp.zeros((tq, 1), F32), jnp.zeros((tq, LANES), F32)),) * 2,
             (sm0(), sm0(), sm0(), sm0()))
    carry = step(i, carry, True)
    mla, fox, sb, dif = lax.fori_loop(0, i, lambda t, c: step(i - 1 - t, c, False), carry)

    wuv = wuv_ref[...]
    oa = [_dot((acc / l).astype(BF16), wuv) for (_, l, acc) in mla]
    oa_ref[...] = jnp.where(lo, oa[0], oa[1]).astype(BF16)
    ob_ref[...] = jnp.where(lo, fox[0][2] / fox[0][1], fox[1][2] / fox[1][1]).astype(BF16)
    oc_ref[...] = jnp.where(lo, sb[0][1], sb[1][1]).astype(BF16)
    lam = _diff_lambda(lam_ref, lam_init)
    dn = [acc / l for (_, l, acc) in dif]
    od = jnp.where(lo, dn[0] - lam * dn[1], dn[2] - lam * dn[3])
    od2 = od * od
    ss0 = jnp.sum(jnp.where(lo, od2, 0.0), axis=-1, keepdims=True)
    ss1 = jnp.sum(jnp.where(lo, 0.0, od2), axis=-1, keepdims=True)
    inv = jnp.where(lo, lax.rsqrt(ss0 * (1.0 / HD) + EPS), lax.rsqrt(ss1 * (1.0 / HD) + EPS))
    od_ref[...] = (od * inv * dg_ref[...] * (1.0 - lam_init)).astype(BF16)


def _attn_prompt(qmla, qfsd, kmla, ksh, fcc, fcr, wuv, dg, lamv, batch, seq, tq, lam_init):
    m = qmla.shape[0]
    nq = seq // tq
    npair = HEADS // 2
    qrow = lambda off: pl.BlockSpec((tq, LANES), lambda b, p, i: (b * nq + i, off + p))
    out = pl.BlockSpec((tq, LANES), lambda b, p, i: (b * nq + i, p))
    return pl.pallas_call(
        functools.partial(_attn_prompt_kernel, tq=tq, tk=tq, lam_init=lam_init),
        out_shape=tuple(jax.ShapeDtypeStruct((m, GROUP), BF16) for _ in range(4)),
        grid=(batch, npair, nq),
        in_specs=[
            pl.BlockSpec((tq, 2 * LANES), lambda b, p, i: (b * nq + i, p)),
            qrow(0), qrow(npair), qrow(2 * npair),
            pl.BlockSpec((seq, 2 * LANES), lambda b, p, i: (b, p)),
            pl.BlockSpec((seq, KSH_W), lambda b, p, i: (b, 0)),
            pl.BlockSpec((1, tq, LANES), lambda b, p, i: (b, i, 0)),
            pl.BlockSpec((1, nq, HEADS, tq), lambda b, p, i: (b, 0, 0, 0)),
            pl.BlockSpec((MLA_KV_RANK, LANES), lambda b, p, i: (0, p)),
            pl.BlockSpec((1, LANES), lambda b, p, i: (0, 0)),
            pl.BlockSpec((4, LANES), lambda b, p, i: (0, 0)),
        ],
        out_specs=(out, out, out, out),
        compiler_params=_cparams(("parallel", "parallel", "arbitrary")),
        name="attn_prompt",
    )(qmla, qfsd, qfsd, qfsd, kmla, ksh, fcc, fcr, wuv, dg, lamv)


def _out_memq_kernel(x_ref, oa_ref, ob_ref, oc_ref, od_ref, wout_ref, g_ref, wq_ref, qg_ref, x1_ref, qm_ref):
    x1 = x_ref[...]
    for n, o_ref in enumerate((oa_ref, ob_ref, oc_ref, od_ref)):
        x1 = x1 + _dot(o_ref[...], wout_ref[n * GROUP:(n + 1) * GROUP, :])
    x1_ref[...] = x1
    q = _dot(_rms_rows(x1, g_ref[...]).astype(BF16), wq_ref[...])
    qg = qg_ref[...]
    for hh in range(MEM_HEADS):
        sl = slice(hh * MEM_HD, (hh + 1) * MEM_HD)
        qm_ref[:, sl] = (_rms_rows(q[:, sl], qg) * MEM_SCALE).astype(BF16)


def _out_memq(x, o4, wout, g, wq, qg, tm):
    m, d = x.shape
    const = lambda a: pl.BlockSpec(a.shape, lambda i: (0,) * a.ndim)
    row = lambda w: pl.BlockSpec((tm, w), lambda i: (i, 0))
    return pl.pallas_call(
        _out_memq_kernel,
        out_shape=(jax.ShapeDtypeStruct((m, d), F32), jax.ShapeDtypeStruct((m, MEM_HEADS * MEM_HD), BF16)),
        grid=(m // tm,),
        in_specs=[row(d)] + [row(GROUP)] * 4 + [const(wout), const(g), const(wq), const(qg)],
        out_specs=(row(d), row(MEM_HEADS * MEM_HD)),
        compiler_params=_cparams(("parallel",)),
        name="out_proj_memq",
    )(x, *o4, wout, g, wq, qg)


def _memkv_kernel(mem_ref, g_ref, wk_ref, wv_ref, kg_ref, o_ref):
    mn = _rms_rows(mem_ref[...], g_ref[...]).astype(BF16)
    k = _dot(mn, wk_ref[...])
    kg = kg_ref[...]
    w = MEM_HEADS * MEM_HD
    for hh in range(MEM_HEADS):
        sl = slice(hh * MEM_HD, (hh + 1) * MEM_HD)
        o_ref[:, sl] = _rms_rows(k[:, sl], kg)
    o_ref[:, w:2 * w] = _dot(mn, wv_ref[...])


def _memkv(mem, g, wk, wv, kg, tm):
    m, d = mem.shape
    w = MEM_HEADS * MEM_HD
    const = lambda a: pl.BlockSpec(a.shape, lambda i: (0,) * a.ndim)
    return pl.pallas_call(
        _memkv_kernel,
        out_shape=jax.ShapeDtypeStruct((m, 2 * w), F32),
        grid=(m // tm,),
        in_specs=[pl.BlockSpec((tm, d), lambda i: (i, 0)), const(g), const(wk), const(wv), const(kg)],
        out_specs=pl.BlockSpec((tm, 2 * w), lambda i: (i, 0)),
        compiler_params=_cparams(("parallel",)),
        name="memory_kv",
    )(mem, g, wk, wv, kg)


def _mem_prompt_kernel(x1_ref, qm_ref, kv_ref, wo_ref, o_ref):
    w = MEM_HEADS * MEM_HD
    outs = []
    for hh in range(MEM_HEADS):
        k = kv_ref[:, hh * MEM_HD:(hh + 1) * MEM_HD].astype(BF16)
        v = kv_ref[:, w + hh * MEM_HD:w + (hh + 1) * MEM_HD].astype(BF16)
        s = _dot_nt(qm_ref[:, hh * MEM_HD:(hh + 1) * MEM_HD], k)
        e = jnp.exp(s - jnp.max(s, axis=-1, keepdims=True))
        outs.append((_dot(e.astype(BF16), v) / jnp.sum(e, axis=-1, keepdims=True)).astype(BF16))
    o_ref[...] = x1_ref[...] + _dot(jnp.concatenate(outs, axis=1), wo_ref[...])


def _mem_prompt(x1, qm, mkv, wo, batch, seq, n_mem, tm):
    m, d = x1.shape
    w = MEM_HEADS * MEM_HD
    nt = seq // tm
    return pl.pallas_call(
        _mem_prompt_kernel,
        out_shape=jax.ShapeDtypeStruct((m, d), F32),
        grid=(batch, nt),
        in_specs=[
            pl.BlockSpec((tm, d), lambda b, i: (b * nt + i, 0)),
            pl.BlockSpec((tm, w), lambda b, i: (b * nt + i, 0)),
            pl.BlockSpec((n_mem, 2 * w), lambda b, i: (b, 0)),
            pl.BlockSpec(wo.shape, lambda b, i: (0, 0)),
        ],
        out_specs=pl.BlockSpec((tm, d), lambda b, i: (b * nt + i, 0)),
        compiler_params=_cparams(("parallel", "parallel")),
        name="mem_attn_prompt",
    )(x1, qm, mkv, wo)


def _mem_sample_kernel(x1_ref, qm_ref, kv_ref, wo_ref, o_ref, om_ref, *, gsz):
    w = MEM_HEADS * MEM_HD
    qf = qm_ref[...]
    sub = lax.broadcasted_iota(jnp.int32, (8, w), 0)
    blk = lax.broadcasted_iota(jnp.int32, (8, w), 1) // MEM_HD
    diag = sub == blk
    for g in range(gsz):
        qbd = jnp.where(diag, jnp.broadcast_to(qf[g:g + 1, :], (8, w)), 0.0).astype(BF16)
        s = _dot_nt(qbd, kv_ref[g, :, 0:w].astype(BF16))
        e = jnp.exp(s - jnp.max(s, axis=-1, keepdims=True))
        o = _dot(e.astype(BF16), kv_ref[g, :, w:2 * w].astype(BF16)) / jnp.sum(e, axis=-1, keepdims=True)
        om_ref[g:g + 1, :] = jnp.sum(jnp.where(diag, o, 0.0), axis=0, keepdims=True)
    o_ref[...] = x1_ref[...] + _dot(om_ref[...].astype(BF16), wo_ref[...])


def _mem_sample(x1, qm, kv, wo, gsz):
    m, d = x1.shape
    w = MEM_HEADS * MEM_HD
    n_mem = kv.shape[1]
    return pl.pallas_call(
        functools.partial(_mem_sample_kernel, gsz=gsz),
        out_shape=jax.ShapeDtypeStruct((m, d), F32),
        grid=(m // gsz,),
        in_specs=[
            pl.BlockSpec((gsz, d), lambda i: (i, 0)),
            pl.BlockSpec((gsz, w), lambda i: (i, 0)),
            pl.BlockSpec((gsz, n_mem, 2 * w), lambda i: (i, 0, 0)),
            pl.BlockSpec(wo.shape, lambda i: (0, 0)),
        ],
        out_specs=pl.BlockSpec((gsz, d), lambda i: (i, 0)),
        scratch_shapes=[pltpu.VMEM((gsz, w), F32)],
        compiler_params=_cparams(("parallel",)),
        name="mem_attn_sample",
    )(x1, qm, kv, wo)


def _suffix_excl(x):
    n = x.shape[1]
    lane = lax.broadcasted_iota(jnp.int32, x.shape, 1)
    y = x
    sh = 1
    while sh < n:
        y = y + jnp.where(lane + sh < n, pltpu.roll(y, n - sh, 1), 0.0)
        sh *= 2
    return y - x


def _attn_sample_kernel(pt_ref, q8m_ref, fq8_ref, sq8_ref, dq8_ref, kn8_ref, ksn_ref, lfn_ref,
                        wukt_ref, wuv_ref, dg_ref, lam_ref,
                        c_mla, c_fox, c_sb, c_dif,
                        oa_ref, ob_ref, oc_ref, od_ref,
                        b_mla, b_fox, b_sb, b_dif, sems,
                        st_m, st_l, st_acc, st_r,
                        *, layer, ppc, kb, page, lam_init):
    s_id = pl.program_id(0)
    c_id = pl.program_id(1)
    nch = pl.num_programs(1)
    nseq = pl.num_programs(0)
    t = s_id * nch + c_id
    slot = t % 2
    ck = ppc * page
    caches = (c_mla, c_fox, c_sb, c_dif)
    bufs = (b_mla, b_fox, b_sb, b_dif)

    def page_copy(ci, seq, chunk, sl, pg):
        pid = pt_ref[seq, chunk * ppc + pg]
        return pltpu.make_async_copy(caches[ci].at[layer, pid], bufs[ci].at[sl, pl.ds(pg * page, page), :],
                                     sems.at[sl, ci])

    def issue(seq, chunk, sl):
        for pg in range(ppc):
            for ci in range(4):
                page_copy(ci, seq, chunk, sl, pg).start()

    @pl.when(t == 0)
    def _():
        issue(0, nch - 1, 0)

    @pl.when(t + 1 < nseq * nch)
    def _():
        wrap = c_id + 1 == nch
        issue(jnp.where(wrap, s_id + 1, s_id), jnp.where(wrap, nch - 1, nch - 2 - c_id), 1 - slot)

    chunk = nch - 1 - c_id
    for pg in range(ppc):
        for ci in range(4):
            page_copy(ci, s_id, chunk, slot, pg).wait()

    lane = lax.broadcasted_iota(jnp.int32, (HEADS, LANES), 1)
    lo = lane < HD
    z8 = jnp.zeros((HEADS, HD), BF16)
    fq8, sq8, dq8 = fq8_ref[0], sq8_ref[0], dq8_ref[0]
    q8m = q8m_ref[0]
    a_fox = jnp.concatenate([fq8, z8], axis=1)
    a_sb = jnp.concatenate([sq8, z8], axis=1)
    lane64 = lax.broadcasted_iota(jnp.int32, (HEADS, HD), 1)
    zq = jnp.zeros((HEADS, HD), BF16)
    a_dif = jnp.concatenate([jnp.concatenate([jnp.where(lane64 < ROPE_D, dq8, zq), z8], axis=1),
                             jnp.concatenate([jnp.where(lane64 < ROPE_D, zq, dq8), z8], axis=1)], axis=0)
    qn8 = q8m[:, 0:HD]
    sub = lax.broadcasted_iota(jnp.int32, (HEADS, GROUP), 0)
    blk = lax.broadcasted_iota(jnp.int32, (HEADS, GROUP), 1) // HD
    diag = sub == blk
    qn_bd = jnp.where(diag, _tile_lanes(qn8, HEADS), jnp.zeros((HEADS, GROUP), BF16))
    e8 = diag.astype(BF16)
    a_pe = q8m[:, HD:HD + ROPE_D]
    ident_mask = (lax.broadcasted_iota(jnp.int32, (HEADS, HEADS), 0)
                  == lax.broadcasted_iota(jnp.int32, (HEADS, HEADS), 1))
    ident = ident_mask.astype(BF16)

    @pl.when(c_id == 0)
    def _():
        ksn = ksn_ref[0].astype(F32)
        qf = q8m.astype(F32)
        s_m = jnp.sum(qf * kn8_ref[0].astype(F32), axis=-1, keepdims=True)
        kf = ksn[:, LANES:2 * LANES]
        s_f = jnp.sum(jnp.where(lo, a_fox.astype(F32) * kf, 0.0), axis=-1, keepdims=True)
        kd = ksn[:, 5 * LANES:6 * LANES]
        s_d = jnp.sum(a_dif.astype(F32) * kd, axis=-1, keepdims=True)
        st_m[0:8, :] = s_m
        st_m[8:16, :] = s_f
        st_m[16:32, :] = s_d
        st_l[...] = jnp.ones_like(st_l)
        st_acc[0:8, :] = jnp.broadcast_to(ksn[:, 0:LANES], (HEADS, LANES))
        st_acc[8:16, :] = jnp.broadcast_to(kf, (HEADS, LANES))
        st_acc[16:32, :] = jnp.broadcast_to(kd, (2 * HEADS, LANES))
        st_acc[32:40, :] = jnp.zeros((HEADS, LANES), F32)
        st_r[0:8, :] = jnp.zeros((HEADS, 1), F32)
        lfn = lfn_ref[0]
        st_r[8:16, :] = jnp.sum(jnp.where(ident_mask, lfn, 0.0), axis=-1, keepdims=True)

    wukt = wukt_ref[...]

    def sm_update(rows, s, v):
        m = st_m[rows, :]
        m_new = jnp.maximum(m, jnp.max(s, axis=-1, keepdims=True))
        alpha = jnp.exp(m - m_new)
        e = jnp.exp(s - m_new)
        st_l[rows, :] = alpha * st_l[rows, :] + jnp.sum(e, axis=-1, keepdims=True)
        st_acc[rows, :] = alpha * st_acc[rows, :] + _dot(e.astype(BF16), v)
        st_m[rows, :] = m_new

    def sub_block(i, _):
        k0 = pl.multiple_of((ck // kb - 1 - i) * kb, kb)
        rows = pl.ds(k0, kb)
        ckv = b_mla[slot, rows, 0:LANES].astype(BF16)
        kpe = b_mla[slot, rows, LANES:LANES + ROPE_D].astype(BF16)
        knt = _dot_nt(wukt, ckv)
        inv = lax.rsqrt(_dot(e8, (knt * knt).astype(BF16)) * (1.0 / HD) + EPS)
        s = _dot(qn_bd, knt.astype(BF16)) * inv + _dot_nt(a_pe, kpe)
        sm_update(slice(0, 8), s, ckv)
        fkv = b_fox[slot, rows, 0:LANES].astype(BF16)
        lf = b_fox[slot, rows, LANES:LANES + HEADS]
        lfh, lfl = _split2(lf)
        lft = _dot_nt(ident, lfh) + _dot_nt(ident, lfl)
        aft = st_r[8:16, :]
        bias = _suffix_excl(lft) + aft
        sm_update(slice(8, 16), _dot_nt(a_fox, fkv) + bias, fkv)
        st_r[8:16, :] = aft + jnp.sum(lft, axis=-1, keepdims=True)
        skv = b_sb[slot, rows, :].astype(BF16)
        z = _dot_nt(a_sb, skv)
        lsp = _log_sigmoid(z)
        lsn = lsp - z
        rs = st_r[0:8, :]
        w = jnp.exp(lsp + _suffix_excl(lsn) + rs)
        st_acc[32:40, :] = st_acc[32:40, :] + _dot(w.astype(BF16), skv)
        st_r[0:8, :] = rs + jnp.sum(lsn, axis=-1, keepdims=True)
        dkv = b_dif[slot, rows, :].astype(BF16)
        sm_update(slice(16, 32), _dot_nt(a_dif, dkv), dkv)
        return 0

    lax.fori_loop(0, ck // kb, sub_block, 0)

    @pl.when(c_id == nch - 1)
    def _():
        acc = st_acc[...]
        l = st_l[...]
        olat = (acc[0:8] / l[0:8]).astype(BF16)
        r = _dot(olat, wuv_ref[...])
        oa_ref[0] = jnp.sum(jnp.where(diag, r, 0.0), axis=0, keepdims=True)
        ob_ref[0] = acc[8:16] / l[8:16]
        oc_ref[0] = acc[32:40]
        lam = _diff_lambda(lam_ref, lam_init)
        od = acc[16:24] / l[16:24] - lam * (acc[24:32] / l[24:32])
        ss = jnp.sum(jnp.where(lo, 0.0, od * od), axis=-1, keepdims=True)
        od_ref[0] = od * lax.rsqrt(ss * (1.0 / HD) + EPS) * dg_ref[...] * (1.0 - lam_init)


def _attn_sample(page_table, q8m, fq8, sq8, dq8, kn8, ksn, lfn, wukt, wuv, dg, lamv, caches, layer, ppc, kb, lam_init):
    ns = q8m.shape[0]
    n_pages = page_table.shape[1]
    page = caches[0].shape[2]
    nch = n_pages // ppc
    ck = ppc * page
    seq3 = lambda a: pl.BlockSpec((1,) + a.shape[1:], lambda s, c, pt: (s, 0, 0))
    const = lambda a: pl.BlockSpec(a.shape, lambda s, c, pt: (0,) * a.ndim)
    anyspec = pl.BlockSpec(memory_space=pl.ANY)
    out8 = pl.BlockSpec((1, HEADS, LANES), lambda s, c, pt: (s, 0, 0))
    grid_spec = pltpu.PrefetchScalarGridSpec(
        num_scalar_prefetch=1,
        grid=(ns, nch),
        in_specs=[seq3(q8m), seq3(fq8), seq3(sq8), seq3(dq8), seq3(kn8), seq3(ksn), seq3(lfn),
                  const(wukt), const(wuv), const(dg), const(lamv), anyspec, anyspec, anyspec, anyspec],
        out_specs=(pl.BlockSpec((1, 1, GROUP), lambda s, c, pt: (s, 0, 0)), out8, out8, out8),
        scratch_shapes=[
            pltpu.VMEM((2, ck, caches[0].shape[3]), F32),
            pltpu.VMEM((2, ck, caches[1].shape[3]), F32),
            pltpu.VMEM((2, ck, caches[2].shape[3]), F32),
            pltpu.VMEM((2, ck, caches[3].shape[3]), F32),
            pltpu.SemaphoreType.DMA((2, 4)),
            pltpu.VMEM((32, 1), F32), pltpu.VMEM((32, 1), F32), pltpu.VMEM((40, LANES), F32),
            pltpu.VMEM((16, 1), F32),
        ],
    )
    return pl.pallas_call(
        functools.partial(_attn_sample_kernel, layer=layer, ppc=ppc, kb=kb, page=page, lam_init=lam_init),
        out_shape=(jax.ShapeDtypeStruct((ns, 1, GROUP), F32),) + tuple(
            jax.ShapeDtypeStruct((ns, HEADS, LANES), F32) for _ in range(3)),
        grid_spec=grid_spec,
        compiler_params=_cparams(("arbitrary", "arbitrary")),
        name="attn_sample",
    )(page_table, q8m, fq8, sq8, dq8, kn8, ksn, lfn, wukt, wuv, dg, lamv, *caches)


def _group_matrices():
    def pair(groups):
        w = len(groups)
        e = np.zeros((w, LANES), np.float32)
        for i, g in enumerate(groups):
            if g >= 0:
                e[i, g] = 1.0
        return jnp.asarray(e, BF16), jnp.asarray(e.T.copy(), BF16)
    q_groups = [2 * (i // LANES) + (0 if i % LANES < HD else 1) if i % LANES < HD + ROPE_D else -1
                for i in range(HEADS * LANES)]
    k_groups = [i // LANES if i % LANES < HD else -1 for i in range(HEADS * LANES)]
    g64 = [i // HD for i in range(GROUP)]
    g32 = [i // ROPE_D for i in range(GROUP)]
    out = []
    for g in (q_groups, k_groups, g64, g32):
        out.extend(pair(g))
    return tuple(out)


def _rope_tables(pos):
    half = ROPE_D // 2
    inv = ROPE_THETA ** (-jnp.arange(half, dtype=F32) * (2.0 / ROPE_D))
    ang = pos.astype(F32)[:, None] * inv[None, :]
    cos, sin = jnp.cos(ang), jnp.sin(ang)
    n = pos.shape[0]
    cos32 = jnp.concatenate([cos, cos], axis=1)
    sin32 = jnp.concatenate([-sin, sin], axis=1)
    c_m = jnp.concatenate([jnp.ones((n, HD), F32), cos32, jnp.ones((n, LANES - HD - ROPE_D), F32)], axis=1)
    s_m = jnp.concatenate([jnp.zeros((n, HD), F32), sin32, jnp.zeros((n, LANES - HD - ROPE_D), F32)], axis=1)
    c_d = jnp.tile(cos32, (1, LANES // ROPE_D))
    s_d = jnp.tile(sin32, (1, LANES // ROPE_D))
    return jnp.concatenate([c_m, s_m, c_d, s_d], axis=1)


def _pad_cols(a, w):
    return jnp.pad(a, [(0, 0)] * (a.ndim - 1) + [(0, w - a.shape[-1])])


def _gain_table(P):
    L = P["mla_cq_norm"].shape[0]
    ones32 = jnp.ones((L, ROPE_D), F32)
    zeros32 = jnp.zeros((L, ROPE_D), F32)
    rows = [None] * GV_ROWS
    rows[G_CQ] = P["mla_cq_norm"]
    rows[G_CKV] = P["mla_ckv_norm"]
    rows[G_Q1] = jnp.tile(jnp.concatenate([P["mla_qn_gain"], P["mla_qr_gain"], zeros32], axis=1), (1, HEADS))
    rows[G_Q2] = jnp.tile(jnp.concatenate([P["mla_kn_gain"], ones32, zeros32], axis=1), (1, HEADS))
    rows[G_KR] = P["mla_kr_gain"]
    rows[G_FQ] = jnp.tile(P["fox_q_gain"], (1, HEADS))
    rows[G_FK] = P["fox_k_gain"]
    rows[G_FB] = P["fox_f_bias"]
    rows[G_DQ] = jnp.tile(P["diff_q_gain"], (1, 2 * HEADS))
    rows[G_DK] = jnp.tile(P["diff_k_gain"], (1, 2))
    cnt = jnp.tile(jnp.asarray([[1.0 / HD, 1.0 / ROPE_D]], F32), (L, HEADS))
    rows[G_QCNT] = jnp.concatenate([cnt, jnp.ones((L, LANES - 2 * HEADS), F32)], axis=1)
    rows = [jnp.zeros((L, GV_W), F32) if r is None else _pad_cols(r.astype(F32), GV_W) for r in rows]
    return jnp.stack(rows, axis=1)


def _win_padded(w_in):
    parts, start = [], 0
    for s in IN_SIZES:
        parts.append(w_in[..., start:start + s])
        start += s
    cq, ckv, kpe, fq, fk, fv, fz, sq, sk, sv, dq, dk, dv = parts
    cols = [cq, ckv, _pad_cols(kpe, LANES), fq, fk, fv, _pad_cols(fz, LANES), sq, sk, sv, dq, dk, dv]
    return jnp.concatenate(cols, axis=-1).astype(BF16)


def kernel(x_prompt, x_sample, mem_prompt, cache_mla, cache_fox, cache_sb, cache_diff, cache_mem, page_table, ffn1_norm, ffn1_w1, ffn1_w3, ffn1_w2, mix_norm, w_in, fox_f_bias, mla_cq_norm, mla_ckv_norm, mla_w_uq, mla_w_uk, mla_w_uv, mla_qn_gain, mla_qr_gain, mla_kn_gain, mla_kr_gain, fox_q_gain, fox_k_gain, diff_q_gain, diff_k_gain, diff_lam_q1, diff_lam_k1, diff_lam_q2, diff_lam_k2, diff_sub_gain, w_out, mem_q_norm, mem_kv_norm, mem_w_q, mem_w_k, mem_w_v, mem_q_gain, mem_k_gain, mem_w_o, ffn2_norm, ffn2_w1, ffn2_w3, ffn2_w2):
    batch, seq, d = x_prompt.shape
    ns = x_sample.shape[0]
    depth = w_in.shape[0]
    n_mem = mem_prompt.shape[1]
    n_pages = page_table.shape[1]
    page = cache_mla.shape[2]
    past = n_pages * page
    assert x_sample.shape[1] == 1
    mp = batch * seq

    tm_ffn = min(512, seq)
    tf = 512
    tm = min(256, seq)
    tq = min(256, seq)
    ppc = min(16, n_pages)
    kb = min(512, ppc * page)
    gsz = 8

    bf = lambda a: a.astype(BF16)
    f1 = (bf(ffn1_w1), bf(ffn1_w3), bf(ffn1_w2))
    f2 = (bf(ffn2_w1), bf(ffn2_w3), bf(ffn2_w2))
    win_p = _win_padded(w_in)
    wuq_p = bf(_pad_cols(mla_w_uq.reshape(depth, MLA_Q_RANK, HEADS, HD + ROPE_D), LANES)
               .reshape(depth, MLA_Q_RANK, HEADS * LANES))
    wuk_p = bf(_pad_cols(mla_w_uk, LANES).reshape(depth, MLA_KV_RANK, HEADS * LANES))
    wukt = bf(jnp.transpose(mla_w_uk.reshape(depth, MLA_KV_RANK, GROUP), (0, 2, 1)))
    wuv = bf(mla_w_uv.reshape(depth, MLA_KV_RANK, GROUP))
    wout = bf(w_out)
    wq, wk, wv, wo = bf(mem_w_q), bf(mem_w_k), bf(mem_w_v), bf(mem_w_o)
    P = dict(mla_cq_norm=mla_cq_norm, mla_ckv_norm=mla_ckv_norm, mla_qn_gain=mla_qn_gain, mla_qr_gain=mla_qr_gain,
             mla_kn_gain=mla_kn_gain, mla_kr_gain=mla_kr_gain, fox_q_gain=fox_q_gain, fox_k_gain=fox_k_gain,
             fox_f_bias=fox_f_bias, diff_q_gain=diff_q_gain, diff_k_gain=diff_k_gain)
    gv = _gain_table(P)
    emats = _group_matrices()
    lamv = _pad_cols(jnp.stack([diff_lam_q1, diff_lam_k1, diff_lam_q2, diff_lam_k2], axis=1).astype(F32), LANES)
    dg_pair = jnp.tile(diff_sub_gain.astype(F32), (1, 2))[:, None, :]
    dg_hi = jnp.concatenate([jnp.zeros((depth, 1, HD), F32), diff_sub_gain.astype(F32)[:, None, :]], axis=2)
    row1 = lambda a, l: a[l][None, :].astype(F32)

    tab_p = _rope_tables(jnp.tile(jnp.arange(seq, dtype=jnp.int32), batch))
    tab_s = _rope_tables(jnp.full((ns,), past, jnp.int32))
    caches = (cache_mla, cache_fox, cache_sb, cache_diff)

    xp = x_prompt.reshape(mp, d)
    xs = x_sample.reshape(ns, d)
    memf = mem_prompt.reshape(batch * n_mem, d)
    outs_p = [[] for _ in range(5)]
    outs_s = [[] for _ in range(4)]
    for l in range(depth):
        lam_init = 0.8 - 0.6 * math.exp(-0.3 * l)
        mkv = _memkv(memf, row1(mem_kv_norm, l), wk[l], wv[l], row1(mem_k_gain, l), min(256, n_mem))
        xp = _ffn(xp, row1(ffn1_norm, l), f1[0][l], f1[1][l], f1[2][l], tm_ffn, tf)
        qmla, qfsd, kmla, ksh, r_mla, r_fox, r_sb, r_dif = _proj(
            xp, row1(mix_norm, l), win_p[l], wuq_p[l], wuk_p[l], gv[l], tab_p, emats, tm)
        logf = r_fox[:, 2 * HD:2 * HD + HEADS].reshape(batch, seq, HEADS)
        fcc, fcr = _fc(_pad_cols(logf, LANES), jnp.transpose(logf, (0, 2, 1)), tq)
        o4 = _attn_prompt(qmla, qfsd, kmla, ksh, fcc, fcr, wuv[l], dg_pair[l], lamv[l], batch, seq, tq, lam_init)
        x1, qm = _out_memq(xp, o4, wout[l], row1(mem_q_norm, l), wq[l], row1(mem_q_gain, l), tm)
        x2 = _mem_prompt(x1, qm, mkv, wo[l], batch, seq, n_mem, tm)
        xp = _ffn(x2, row1(ffn2_norm, l), f2[0][l], f2[1][l], f2[2][l], tm_ffn, tf)
        for lst, r in zip(outs_p, (r_mla, r_fox, r_sb, r_dif, mkv)):
            lst.append(r)
        xs = _ffn(xs, row1(ffn1_norm, l), f1[0][l], f1[1][l], f1[2][l], ns, tf)
        qmla, qfsd, kmla, ksh, r_mla, r_fox, r_sb, r_dif = _proj(
            xs, row1(mix_norm, l), win_p[l], wuq_p[l], wuk_p[l], gv[l], tab_s, emats, ns)
        q8m = qmla.reshape(ns, HEADS, LANES)
        fq8 = qfsd[:, 0:GROUP].reshape(ns, HEADS, HD)
        sq8 = qfsd[:, GROUP:2 * GROUP].reshape(ns, HEADS, HD)
        dq8 = qfsd[:, 2 * GROUP:3 * GROUP].reshape(ns, HEADS, HD)
        oa, ob, oc, od = _attn_sample(
            page_table, q8m, fq8, sq8, dq8, kmla.reshape(ns, HEADS, LANES), ksh.reshape(ns, 1, KSH_W),
            r_fox[:, 2 * HD:2 * HD + HEADS].reshape(ns, 1, HEADS), wukt[l], wuv[l], dg_hi[l], lamv[l],
            caches, l, ppc, kb, lam_init)
        hi = lambda a: a[:, :, HD:].reshape(ns, GROUP).astype(BF16)
        o4 = (oa.reshape(ns, GROUP).astype(BF16), hi(ob), hi(oc), hi(od))
        x1, qm = _out_memq(xs, o4, wout[l], row1(mem_q_norm, l), wq[l], row1(mem_q_gain, l), ns)
        x2 = _mem_sample(x1, qm.astype(F32), cache_mem[l].reshape(ns, n_mem, 2 * MEM_HEADS * MEM_HD), wo[l], gsz)
        xs = _ffn(x2, row1(ffn2_norm, l), f2[0][l], f2[1][l], f2[2][l], ns, tf)
        for lst, r in zip(outs_s, (r_mla, r_fox, r_sb, r_dif)):
            lst.append(r)

    stack_p = lambda lst: jnp.stack(lst).reshape(depth, batch, seq, -1)
    stack_s = lambda lst: jnp.stack(lst).reshape(depth, ns, 1, -1)
    return (xp.reshape(batch, seq, d), xs.reshape(ns, 1, d),
            stack_p(outs_p[0]), stack_p(outs_p[1]), stack_p(outs_p[2]), stack_p(outs_p[3]),
            jnp.stack(outs_p[4]).reshape(depth, batch, n_mem, 2, MEM_HEADS, MEM_HD),
            stack_s(outs_s[0]), stack_s(outs_s[1]), stack_s(outs_s[2]), stack_s(outs_s[3]))
```

```python
import functools
import math

import numpy as np
import jax
import jax.numpy as jnp
from jax import lax
from jax.experimental import pallas as pl
from jax.experimental.pallas import tpu as pltpu

F32 = jnp.float32
BF16 = jnp.bfloat16
EPS = 1e-6
ROPE_THETA = 10000.0
NEG_INF = -1e30

LANES = 128
VMEM_LIMIT_BYTES = 56 * 1024 * 1024

HEADS = 8
HD = 64
ROPE_D = 32
MLA_Q_RANK = 384
MLA_KV_RANK = 128
MEM_HEADS = 4
MEM_HD = 128
GROUP = HEADS * HD

MLA_SCALE = (HD + ROPE_D) ** -0.5
FOX_SCALE = HD ** -0.5
SB_SCALE = HD ** -0.5
DIFF_SCALE = ROPE_D ** -0.5
MEM_SCALE = MEM_HD ** -0.5

C_CQ, C_CKV, C_KPE, C_FQ, C_FKV, C_FZ, C_SQ, C_SKV, C_DQ, C_DKV, C_END = (
    0, 384, 512, 640, 1152, 1280, 1408, 1920, 2048, 2560, 2688)
IN_SIZES = (384, 128, 32, 512, 64, 64, 8, 512, 64, 64, 512, 64, 64)

(G_CQ, G_CKV, G_Q1, G_Q2, G_KR, G_FQ, G_FK, G_FB, G_DQ, G_DK, G_QCNT) = range(11)
GV_ROWS = 16
GV_W = 1024


def _cparams(sem):
    return pltpu.CompilerParams(dimension_semantics=sem, vmem_limit_bytes=VMEM_LIMIT_BYTES)


def _dot(a, b):
    return jnp.dot(a, b, preferred_element_type=F32)


def _dot_nt(a, b):
    return lax.dot_general(a, b, (((1,), (1,)), ((), ())), preferred_element_type=F32)


def _rms_rows(x, g):
    return x * lax.rsqrt(jnp.mean(x * x, axis=-1, keepdims=True) + EPS) * g


def _split2(x):
    hi = x.astype(BF16)
    lo = (x - hi.astype(F32)).astype(BF16)
    return hi, lo


def _split3(x):
    h1 = x.astype(BF16)
    r = x - h1.astype(F32)
    h2 = r.astype(BF16)
    h3 = (r - h2.astype(F32)).astype(BF16)
    return h1, h2, h3


def _seg_inv_rms(x, e, et, inv_cnt):
    hi, lo = _split2(x * x)
    ss = _dot(hi, e) + _dot(lo, e)
    inv = lax.rsqrt(ss * inv_cnt + EPS)
    ihi, ilo = _split2(inv)
    return _dot(ihi, et) + _dot(ilo, et)


def _log_sigmoid(z):
    return jnp.minimum(z, 0.0) - jnp.log1p(jnp.exp(-jnp.abs(z)))


def _swap_halves(x):
    w = x.shape[1]
    lane = lax.broadcasted_iota(jnp.int32, x.shape, 1)
    return jnp.where((lane % ROPE_D) < (ROPE_D // 2), pltpu.roll(x, w - ROPE_D // 2, 1),
                     pltpu.roll(x, ROPE_D // 2, 1))


def _tile_lanes(t, n):
    return jnp.concatenate([t] * n, axis=1)


def _ffn_kernel(x_ref, g_ref, w1_ref, w3_ref, w2_ref, o_ref, h_ref, acc_ref):
    f = pl.program_id(1)

    @pl.when(f == 0)
    def _():
        h_ref[...] = _rms_rows(x_ref[...], g_ref[...]).astype(BF16)
        acc_ref[...] = jnp.zeros_like(acc_ref)

    h = h_ref[...]
    a = _dot(h, w1_ref[...])
    b = _dot(h, w3_ref[...])
    act = (a * jax.nn.sigmoid(a)) * b
    acc_ref[...] += _dot(act.astype(BF16), w2_ref[...])

    @pl.when(f == pl.num_programs(1) - 1)
    def _():
        o_ref[...] = x_ref[...] + 0.5 * acc_ref[...]


def _ffn(x, g, w1, w3, w2, tm, tf):
    m, d = x.shape
    dff = w1.shape[1]
    return pl.pallas_call(
        _ffn_kernel,
        out_shape=jax.ShapeDtypeStruct((m, d), F32),
        grid=(m // tm, dff // tf),
        in_specs=[
            pl.BlockSpec((tm, d), lambda i, f: (i, 0)),
            pl.BlockSpec((1, d), lambda i, f: (0, 0)),
            pl.BlockSpec((d, tf), lambda i, f: (0, f)),
            pl.BlockSpec((d, tf), lambda i, f: (0, f)),
            pl.BlockSpec((tf, d), lambda i, f: (f, 0)),
        ],
        out_specs=pl.BlockSpec((tm, d), lambda i, f: (i, 0)),
        scratch_shapes=[pltpu.VMEM((tm, d), BF16), pltpu.VMEM((tm, d), F32)],
        compiler_params=_cparams(("parallel", "arbitrary")),
        name="ffn",
    )(x, g, w1, w3, w2)


def _proj_kernel(x_ref, g_ref, win_ref, wuq_ref, wuk_ref, gv_ref, tab_ref,
                 eq_ref, etq_ref, ek_ref, etk_ref, e64_ref, et64_ref, e32_ref, et32_ref,
                 qmla_ref, qfsd_ref, kmla_ref, ksh_ref, mla_row_ref, fox_row_ref, sb_row_ref, diff_row_ref):
    h = _rms_rows(x_ref[...], g_ref[...]).astype(BF16)
    u = _dot(h, win_ref[...])
    tab = tab_ref[...]
    c_m, s_m = tab[:, 0:LANES], tab[:, LANES:2 * LANES]
    c_d, s_d = tab[:, 2 * LANES:3 * LANES], tab[:, 3 * LANES:4 * LANES]
    lane = lax.broadcasted_iota(jnp.int32, (u.shape[0], LANES), 1)
    lo64 = lane < HD

    def rope(x, c, s):
        n = x.shape[1] // LANES
        return x * _tile_lanes(c, n) + _swap_halves(x) * _tile_lanes(s, n)

    cqn = _rms_rows(u[:, C_CQ:C_CKV], gv_ref[G_CQ:G_CQ + 1, 0:MLA_Q_RANK]).astype(BF16)
    q = _dot(cqn, wuq_ref[...])
    inv = _seg_inv_rms(q, eq_ref[...], etq_ref[...], gv_ref[G_QCNT:G_QCNT + 1, 0:LANES])
    qn = q * inv * gv_ref[G_Q1:G_Q1 + 1, :] * gv_ref[G_Q2:G_Q2 + 1, :]
    qmla_ref[...] = (rope(qn, c_m, s_m) * MLA_SCALE).astype(BF16)

    ckvn = _rms_rows(u[:, C_CKV:C_KPE], gv_ref[G_CKV:G_CKV + 1, 0:LANES])
    kp = u[:, C_KPE:C_FQ]
    kpn = kp * lax.rsqrt(jnp.sum(kp * kp, axis=-1, keepdims=True) * (1.0 / ROPE_D) + EPS) \
        * gv_ref[G_KR:G_KR + 1, 0:LANES]
    kpr = rope(kpn, c_d, s_d)
    mla_row_ref[:, 0:LANES] = ckvn
    mla_row_ref[:, LANES:LANES + ROPE_D] = kpr[:, 0:ROPE_D]
    ckvb = ckvn.astype(BF16)
    kn = _dot(ckvb, wuk_ref[...])
    kns = kn * _seg_inv_rms(kn, ek_ref[...], etk_ref[...], 1.0 / HD)
    kmla_ref[...] = (kns + _tile_lanes(pltpu.roll(kpr, HD, 1), HEADS)).astype(BF16)

    fq = u[:, C_FQ:C_FKV]
    fqn = fq * _seg_inv_rms(fq, e64_ref[...], et64_ref[...], 1.0 / HD) * gv_ref[G_FQ:G_FQ + 1, 0:GROUP]
    fkv = u[:, C_FKV:C_FZ]
    ssf = jnp.sum(jnp.where(lo64, fkv * fkv, 0.0), axis=-1, keepdims=True)
    fkvn = jnp.where(lo64, fkv * lax.rsqrt(ssf * (1.0 / HD) + EPS) * gv_ref[G_FK:G_FK + 1, 0:LANES], fkv)
    logf = _log_sigmoid(u[:, C_FZ:C_SQ] + gv_ref[G_FB:G_FB + 1, 0:LANES])
    fox_row_ref[:, 0:LANES] = fkvn
    fox_row_ref[:, LANES:LANES + HEADS] = logf[:, 0:HEADS]

    skv = u[:, C_SKV:C_DQ]
    sb_row_ref[...] = skv

    dq = u[:, C_DQ:C_DKV]
    dqn = dq * _seg_inv_rms(dq, e32_ref[...], et32_ref[...], 1.0 / ROPE_D) * gv_ref[G_DQ:G_DQ + 1, 0:GROUP]
    dkv = u[:, C_DKV:C_END]
    d2 = dkv * dkv
    ss0 = jnp.sum(jnp.where(lane < ROPE_D, d2, 0.0), axis=-1, keepdims=True)
    ss1 = jnp.sum(jnp.where(jnp.logical_and(lane >= ROPE_D, lo64), d2, 0.0), axis=-1, keepdims=True)
    invd = jnp.where(lane < ROPE_D, lax.rsqrt(ss0 * (1.0 / ROPE_D) + EPS), lax.rsqrt(ss1 * (1.0 / ROPE_D) + EPS))
    dkr = rope(dkv * invd * gv_ref[G_DK:G_DK + 1, 0:LANES], c_d, s_d)
    dkvn = jnp.where(lo64, dkr, dkv)
    diff_row_ref[...] = dkvn

    qfsd_ref[:, 0:GROUP] = (fqn * FOX_SCALE).astype(BF16)
    qfsd_ref[:, GROUP:2 * GROUP] = (u[:, C_SQ:C_SKV] * SB_SCALE).astype(BF16)
    qfsd_ref[:, 2 * GROUP:3 * GROUP] = (rope(dqn, c_d, s_d) * DIFF_SCALE).astype(BF16)

    ksh_ref[:, 0:LANES] = ckvb
    for k, v in enumerate((fkvn, skv, dkvn)):
        ksh_ref[:, (1 + 2 * k) * LANES:(2 + 2 * k) * LANES] = v.astype(BF16)
        ksh_ref[:, (2 + 2 * k) * LANES:(3 + 2 * k) * LANES] = pltpu.roll(v, HD, 1).astype(BF16)


KSH_W = 7 * LANES


def _proj(x, g, win, wuq, wuk, gv, tab, emats, tm):
    m, d = x.shape
    const = lambda a: pl.BlockSpec(a.shape, lambda i: (0,) * a.ndim)
    row = lambda w: pl.BlockSpec((tm, w), lambda i: (i, 0))
    out_w = (8 * LANES, 3 * GROUP, 8 * LANES, KSH_W, 160, 136, 128, 128)
    out_dt = (BF16, BF16, BF16, BF16, F32, F32, F32, F32)
    return pl.pallas_call(
        _proj_kernel,
        out_shape=tuple(jax.ShapeDtypeStruct((m, w), t) for w, t in zip(out_w, out_dt)),
        grid=(m // tm,),
        in_specs=[row(d), const(g), const(win), const(wuq), const(wuk), const(gv), row(4 * LANES)]
                 + [const(e) for e in emats],
        out_specs=tuple(row(w) for w in out_w),
        compiler_params=_cparams(("parallel",)),
        name="mixer_proj",
    )(x, g, win, wuq, wuk, gv, tab, *emats)


def _fc_kernel(lc_ref, lr_ref, fcc_ref, fcr_ref, *, tb):
    s = lc_ref.shape[1]
    r = lax.broadcasted_iota(jnp.int32, (tb, tb), 0)
    c = lax.broadcasted_iota(jnp.int32, (tb, tb), 1)
    tril = (c <= r).astype(BF16)
    triu = (r <= c).astype(BF16)
    carry_c = jnp.zeros((1, LANES), F32)
    carry_r = jnp.zeros((HEADS, 1), F32)
    for j in range(s // tb):
        xc = lc_ref[0, j * tb:(j + 1) * tb, :]
        cs = sum(_dot(tril, p) for p in _split3(xc)) + carry_c
        fcc_ref[0, j * tb:(j + 1) * tb, :] = cs
        carry_c = cs[tb - 1:tb, :]
        xr = lr_ref[0, :, j * tb:(j + 1) * tb]
        csr = sum(_dot(p, triu) for p in _split3(xr)) + carry_r
        fcr_ref[0, j] = csr
        carry_r = csr[:, tb - 1:tb]


def _fc(logf_col, logf_row, tb):
    b, s, _ = logf_col.shape
    return pl.pallas_call(
        functools.partial(_fc_kernel, tb=tb),
        out_shape=(jax.ShapeDtypeStruct((b, s, LANES), F32), jax.ShapeDtypeStruct((b, s // tb, HEADS, tb), F32)),
        grid=(b,),
        in_specs=[pl.BlockSpec((1, s, LANES), lambda i: (i, 0, 0)), pl.BlockSpec((1, HEADS, s), lambda i: (i, 0, 0))],
        out_specs=(pl.BlockSpec((1, s, LANES), lambda i: (i, 0, 0)),
                   pl.BlockSpec((1, s // tb, HEADS, tb), lambda i: (i, 0, 0, 0))),
        compiler_params=_cparams(("parallel",)),
        name="forget_prefix",
    )(logf_col, logf_row)


def _diff_lambda(lam_ref, lam_init):
    a = jnp.sum(lam_ref[0:1, :] * lam_ref[1:2, :], axis=-1, keepdims=True)
    b = jnp.sum(lam_ref[2:3, :] * lam_ref[3:4, :], axis=-1, keepdims=True)
    return jnp.exp(a) - jnp.exp(b) + lam_init


def _attn_prompt_kernel(qm_ref, fq_ref, sq_ref, dq_ref, km_ref, ks_ref, fcc_ref, fcr_ref, wuv_ref, dg_ref, lam_ref,
                        oa_ref, ob_ref, oc_ref, od_ref, *, tq, tk, lam_init):
    p = pl.program_id(1)
    i = pl.program_id(2)
    lane = lax.broadcasted_iota(jnp.int32, (tq, LANES), 1)
    lo = lane < HD
    fqp, sqp, dqp = (r[...].astype(F32) for r in (fq_ref, sq_ref, dq_ref))
    q_mla = (qm_ref[:, 0:LANES], qm_ref[:, LANES:2 * LANES])
    q_fox = (jnp.where(lo, fqp, 0.0).astype(BF16), jnp.where(lo, 0.0, fqp).astype(BF16))
    q_sb = (jnp.where(lo, sqp, 0.0).astype(BF16), jnp.where(lo, 0.0, sqp).astype(BF16))
    q_dif = tuple(jnp.where(jnp.logical_and(lane >= ROPE_D * c, lane < ROPE_D * (c + 1)), dqp, 0.0).astype(BF16)
                  for c in range(4))

    fcc = fcc_ref[0]
    fcq = tuple(jnp.sum(jnp.where(lane == 2 * p + r, fcc, 0.0), axis=-1, keepdims=True) for r in range(2))

    rr = lax.broadcasted_iota(jnp.int32, (tk, tk), 0)
    cc = lax.broadcasted_iota(jnp.int32, (tk, tk), 1)
    t_after = (rr > cc).astype(BF16)

    def softmax_update(st, q, k, v, bias, mask):
        m, l, acc = st
        s = _dot_nt(q, k)
        if bias is not None:
            s = s + bias
        if mask is not None:
            s = jnp.where(mask, s, NEG_INF)
        m_new = jnp.maximum(m, jnp.max(s, axis=-1, keepdims=True))
        alpha = jnp.exp(m - m_new)
        e = jnp.exp(s - m_new)
        l = alpha * l + jnp.sum(e, axis=-1, keepdims=True)
        acc = alpha * acc + _dot(e.astype(BF16), v)
        return m_new, l, acc

    def sb_update(st, q, k, v, strict):
        r_sum, acc = st
        z = _dot_nt(q, k)
        lsp = _log_sigmoid(z)
        lsn = lsp - z
        if strict is not None:
            lsn = jnp.where(strict, lsn, 0.0)
        hi, lo2 = _split2(lsn)
        cs = _dot(hi, t_after) + _dot(lo2, t_after)
        w = jnp.exp(lsp + cs + r_sum)
        if strict is not None:
            w = jnp.where(strict, w, 0.0)
        acc = acc + _dot(w.astype(BF16), v)
        return r_sum + jnp.sum(lsn, axis=-1, keepdims=True), acc

    def step(j, carry, masked):
        mla, fox, sb, dif = carry
        ks0 = pl.multiple_of(j * tk, tk)
        kmb = km_ref[pl.ds(ks0, tk), :]
        ksb = ks_ref[pl.ds(ks0, tk), :]
        seg = lambda n: ksb[:, n * LANES:(n + 1) * LANES]
        ckv, kf, kfs, ksk, kss, kd, kds = (seg(n) for n in range(7))
        causal = strict = None
        if masked:
            row = lax.broadcasted_iota(jnp.int32, (tq, tk), 0)
            col = lax.broadcasted_iota(jnp.int32, (tq, tk), 1)
            causal, strict = col <= row, col < row
        mla = tuple(softmax_update(mla[r], q_mla[r], kmb[:, r * LANES:(r + 1) * LANES], ckv, None, causal)
                    for r in range(2))
        fck = tuple(fcr_ref[0, j, pl.ds(2 * p + r, 1), :] for r in range(2))
        fox = (softmax_update(fox[0], q_fox[0], kf, kfs, fcq[0] - fck[0], causal),
               softmax_update(fox[1], q_fox[1], kfs, kf, fcq[1] - fck[1], causal))
        sb = (sb_update(sb[0], q_sb[0], ksk, kss, strict), sb_update(sb[1], q_sb[1], kss, ksk, strict))
        dif = (softmax_update(dif[0], q_dif[0], kd, kds, None, causal),
               softmax_update(dif[1], q_dif[1], kd, kds, None, causal),
               softmax_update(dif[2], q_dif[2], kds, kd, None, causal),
               softmax_update(dif[3], q_dif[3], kds, kd, None, causal))
        return mla, fox, sb, dif

    def sm0():
        return (jnp.full((tq, 1), NEG_INF, F32), jnp.zeros((tq, 1), F32), jnp.zeros((tq, LANES), F32))

    carry = ((sm0(), sm0()), (sm0(), sm0()),
             ((jnp.zeros((tq, 1), F32), jnp.zeros((tq, LANES), F32)),) * 2,
             (sm0(), sm0(), sm0(), sm0()))
    carry = step(i, carry, True)
    mla, fox, sb, dif = lax.fori_loop(0, i, lambda t, c: step(i - 1 - t, c, False), carry)

    wuv = wuv_ref[...]
    oa = [_dot((acc / l).astype(BF16), wuv) for (_, l, acc) in mla]
    oa_ref[...] = jnp.where(lo, oa[0], oa[1]).astype(BF16)
    ob_ref[...] = jnp.where(lo, fox[0][2] / fox[0][1], fox[1][2] / fox[1][1]).astype(BF16)
    oc_ref[...] = jnp.where(lo, sb[0][1], sb[1][1]).astype(BF16)
    lam = _diff_lambda(lam_ref, lam_init)
    dn = [acc / l for (_, l, acc) in dif]
    od = jnp.where(lo, dn[0] - lam * dn[1], dn[2] - lam * dn[3])
    od2 = od * od
    ss0 = jnp.sum(jnp.where(lo, od2, 0.0), axis=-1, keepdims=True)
    ss1 = jnp.sum(jnp.where(lo, 0.0, od2), axis=-1, keepdims=True)
    inv = jnp.where(lo, lax.rsqrt(ss0 * (1.0 / HD) + EPS), lax.rsqrt(ss1 * (1.0 / HD) + EPS))
    od_ref[...] = (od * inv * dg_ref[...] * (1.0 - lam_init)).astype(BF16)


def _attn_prompt(qmla, qfsd, kmla, ksh, fcc, fcr, wuv, dg, lamv, batch, seq, tq, lam_init):
    m = qmla.shape[0]
    nq = seq // tq
    npair = HEADS // 2
    qrow = lambda off: pl.BlockSpec((tq, LANES), lambda b, p, i: (b * nq + i, off + p))
    out = pl.BlockSpec((tq, LANES), lambda b, p, i: (b * nq + i, p))
    return pl.pallas_call(
        functools.partial(_attn_prompt_kernel, tq=tq, tk=tq, lam_init=lam_init),
        out_shape=tuple(jax.ShapeDtypeStruct((m, GROUP), BF16) for _ in range(4)),
        grid=(batch, npair, nq),
        in_specs=[
            pl.BlockSpec((tq, 2 * LANES), lambda b, p, i: (b * nq + i, p)),
            qrow(0), qrow(npair), qrow(2 * npair),
            pl.BlockSpec((seq, 2 * LANES), lambda b, p, i: (b, p)),
            pl.BlockSpec((seq, KSH_W), lambda b, p, i: (b, 0)),
            pl.BlockSpec((1, tq, LANES), lambda b, p, i: (b, i, 0)),
            pl.BlockSpec((1, nq, HEADS, tq), lambda b, p, i: (b, 0, 0, 0)),
            pl.BlockSpec((MLA_KV_RANK, LANES), lambda b, p, i: (0, p)),
            pl.BlockSpec((1, LANES), lambda b, p, i: (0, 0)),
            pl.BlockSpec((4, LANES), lambda b, p, i: (0, 0)),
        ],
        out_specs=(out, out, out, out),
        compiler_params=_cparams(("parallel", "parallel", "arbitrary")),
        name="attn_prompt",
    )(qmla, qfsd, qfsd, qfsd, kmla, ksh, fcc, fcr, wuv, dg, lamv)


def _out_memq_kernel(x_ref, oa_ref, ob_ref, oc_ref, od_ref, wout_ref, g_ref, wq_ref, qg_ref, x1_ref, qm_ref):
    x1 = x_ref[...]
    for n, o_ref in enumerate((oa_ref, ob_ref, oc_ref, od_ref)):
        x1 = x1 + _dot(o_ref[...], wout_ref[n * GROUP:(n + 1) * GROUP, :])
    x1_ref[...] = x1
    q = _dot(_rms_rows(x1, g_ref[...]).astype(BF16), wq_ref[...])
    qg = qg_ref[...]
    for hh in range(MEM_HEADS):
        sl = slice(hh * MEM_HD, (hh + 1) * MEM_HD)
        qm_ref[:, sl] = (_rms_rows(q[:, sl], qg) * MEM_SCALE).astype(BF16)


def _out_memq(x, o4, wout, g, wq, qg, tm):
    m, d = x.shape
    const = lambda a: pl.BlockSpec(a.shape, lambda i: (0,) * a.ndim)
    row = lambda w: pl.BlockSpec((tm, w), lambda i: (i, 0))
    return pl.pallas_call(
        _out_memq_kernel,
        out_shape=(jax.ShapeDtypeStruct((m, d), F32), jax.ShapeDtypeStruct((m, MEM_HEADS * MEM_HD), BF16)),
        grid=(m // tm,),
        in_specs=[row(d)] + [row(GROUP)] * 4 + [const(wout), const(g), const(wq), const(qg)],
        out_specs=(row(d), row(MEM_HEADS * MEM_HD)),
        compiler_params=_cparams(("parallel",)),
        name="out_proj_memq",
    )(x, *o4, wout, g, wq, qg)


def _memkv_kernel(mem_ref, g_ref, wk_ref, wv_ref, kg_ref, o_ref):
    mn = _rms_rows(mem_ref[...], g_ref[...]).astype(BF16)
    k = _dot(mn, wk_ref[...])
    kg = kg_ref[...]
    w = MEM_HEADS * MEM_HD
    for hh in range(MEM_HEADS):
        sl = slice(hh * MEM_HD, (hh + 1) * MEM_HD)
        o_ref[:, sl] = _rms_rows(k[:, sl], kg)
    o_ref[:, w:2 * w] = _dot(mn, wv_ref[...])


def _memkv(mem, g, wk, wv, kg, tm):
    m, d = mem.shape
    w = MEM_HEADS * MEM_HD
    const = lambda a: pl.BlockSpec(a.shape, lambda i: (0,) * a.ndim)
    return pl.pallas_call(
        _memkv_kernel,
        out_shape=jax.ShapeDtypeStruct((m, 2 * w), F32),
        grid=(m // tm,),
        in_specs=[pl.BlockSpec((tm, d), lambda i: (i, 0)), const(g), const(wk), const(wv), const(kg)],
        out_specs=pl.BlockSpec((tm, 2 * w), lambda i: (i, 0)),
        compiler_params=_cparams(("parallel",)),
        name="memory_kv",
    )(mem, g, wk, wv, kg)


def _mem_prompt_kernel(x1_ref, qm_ref, kv_ref, wo_ref, o_ref):
    w = MEM_HEADS * MEM_HD
    outs = []
    for hh in range(MEM_HEADS):
        k = kv_ref[:, hh * MEM_HD:(hh + 1) * MEM_HD].astype(BF16)
        v = kv_ref[:, w + hh * MEM_HD:w + (hh + 1) * MEM_HD].astype(BF16)
        s = _dot_nt(qm_ref[:, hh * MEM_HD:(hh + 1) * MEM_HD], k)
        e = jnp.exp(s - jnp.max(s, axis=-1, keepdims=True))
        outs.append((_dot(e.astype(BF16), v) / jnp.sum(e, axis=-1, keepdims=True)).astype(BF16))
    o_ref[...] = x1_ref[...] + _dot(jnp.concatenate(outs, axis=1), wo_ref[...])


def _mem_prompt(x1, qm, mkv, wo, batch, seq, n_mem, tm):
    m, d = x1.shape
    w = MEM_HEADS * MEM_HD
    nt = seq // tm
    return pl.pallas_call(
        _mem_prompt_kernel,
        out_shape=jax.ShapeDtypeStruct((m, d), F32),
        grid=(batch, nt),
        in_specs=[
            pl.BlockSpec((tm, d), lambda b, i: (b * nt + i, 0)),
            pl.BlockSpec((tm, w), lambda b, i: (b * nt + i, 0)),
            pl.BlockSpec((n_mem, 2 * w), lambda b, i: (b, 0)),
            pl.BlockSpec(wo.shape, lambda b, i: (0, 0)),
        ],
        out_specs=pl.BlockSpec((tm, d), lambda b, i: (b * nt + i, 0)),
        compiler_params=_cparams(("parallel", "parallel")),
        name="mem_attn_prompt",
    )(x1, qm, mkv, wo)


def _mem_sample_kernel(x1_ref, qm_ref, kv_ref, wo_ref, o_ref, om_ref, *, gsz):
    for g in range(gsz):
        q = qm_ref[g]
        s = jnp.sum(kv_ref[g, :, 0] * q[None], axis=-1, keepdims=True)
        e = jnp.exp(s - jnp.max(s, axis=0, keepdims=True))
        o = jnp.sum(e * kv_ref[g, :, 1], axis=0) / jnp.sum(e, axis=0)
        for hh in range(MEM_HEADS):
            om_ref[hh, g:g + 1, :] = o[hh:hh + 1, :]
    acc = x1_ref[...]
    for hh in range(MEM_HEADS):
        acc = acc + _dot(om_ref[hh].astype(BF16), wo_ref[hh * MEM_HD:(hh + 1) * MEM_HD, :])
    o_ref[...] = acc


def _mem_sample(x1, qm, kv, layer, wo, gsz):
    m, d = x1.shape
    n_mem = kv.shape[2]
    return pl.pallas_call(
        functools.partial(_mem_sample_kernel, gsz=gsz),
        out_shape=jax.ShapeDtypeStruct((m, d), F32),
        grid=(m // gsz,),
        in_specs=[
            pl.BlockSpec((gsz, d), lambda i: (i, 0)),
            pl.BlockSpec((gsz, MEM_HEADS, MEM_HD), lambda i: (i, 0, 0)),
            pl.BlockSpec((None, gsz, n_mem, 2, MEM_HEADS, MEM_HD), lambda i: (layer, i, 0, 0, 0, 0)),
            pl.BlockSpec(wo.shape, lambda i: (0, 0)),
        ],
        out_specs=pl.BlockSpec((gsz, d), lambda i: (i, 0)),
        scratch_shapes=[pltpu.VMEM((MEM_HEADS, gsz, MEM_HD), F32)],
        compiler_params=_cparams(("parallel",)),
        name="mem_attn_sample",
    )(x1, qm, kv, wo)


def _suffix_excl(x, t_after):
    nb = x.shape[1] // LANES
    xs = jnp.concatenate([x[:, j * LANES:(j + 1) * LANES] for j in range(nb)], axis=0)
    hi, lo = _split2(xs)
    ys = _dot(hi, t_after) + _dot(lo, t_after)
    tot = jnp.sum(xs, axis=-1, keepdims=True)
    out = [None] * nb
    carry = jnp.zeros((x.shape[0], 1), F32)
    for j in range(nb - 1, -1, -1):
        out[j] = ys[j * 8:(j + 1) * 8, :] + carry
        carry = carry + tot[j * 8:(j + 1) * 8, :]
    return jnp.concatenate(out, axis=1), carry


def _attn_sample_kernel(pt_ref, q8m_ref, qcol_ref, fq8_ref, sq8_ref, dq8_ref, kn8_ref, ksn_ref, lfn_ref,
                        wukt_ref, wuv_ref, dg_ref, lam_ref,
                        c_mla, c_fox, c_sb, c_dif,
                        oa_ref, ob_ref, oc_ref, od_ref,
                        b_mla, b_fox, b_sb, b_dif, sems,
                        m_a, l_a, acc_a, m_f, l_f, acc_f, aft_f, r_s, acc_s, m_d, l_d, acc_d,
                        *, layer, ppc, kb, page, lam_init):
    sm_a, sm_f, sm_d = (m_a, l_a, acc_a), (m_f, l_f, acc_f), (m_d, l_d, acc_d)
    s_id = pl.program_id(0)
    c_id = pl.program_id(1)
    nch = pl.num_programs(1)
    nseq = pl.num_programs(0)
    t = s_id * nch + c_id
    slot = t % 2
    ck = ppc * page
    caches = (c_mla, c_fox, c_sb, c_dif)
    bufs = (b_mla, b_fox, b_sb, b_dif)

    def issue(seq, chunk, sl):
        for pg in range(ppc):
            pid = pt_ref[seq, chunk * ppc + pg]
            keys = pl.ds(pg * page, page)
            for ci in range(4):
                dst = bufs[ci].at[sl, :, keys] if ci < 2 else bufs[ci].at[sl, keys, :]
                pltpu.make_async_copy(caches[ci].at[layer, pid], dst, sems.at[sl, ci]).start()

    @pl.when(t == 0)
    def _():
        issue(0, nch - 1, 0)

    @pl.when(t + 1 < nseq * nch)
    def _():
        wrap = c_id + 1 == nch
        issue(jnp.where(wrap, s_id + 1, s_id), jnp.where(wrap, nch - 1, nch - 2 - c_id), 1 - slot)

    for ci in range(4):
        pltpu.make_async_copy(bufs[ci].at[slot], bufs[ci].at[slot], sems.at[slot, ci]).wait()

    lane = lax.broadcasted_iota(jnp.int32, (HEADS, LANES), 1)
    lo = lane < HD
    z8 = jnp.zeros((HEADS, HD), BF16)
    fq8, sq8, dq8 = fq8_ref[0], sq8_ref[0], dq8_ref[0]
    q8m = q8m_ref[0]
    a_sb = jnp.concatenate([sq8, z8], axis=1)
    lane64 = lax.broadcasted_iota(jnp.int32, (HEADS, HD), 1)
    zq = jnp.zeros((HEADS, HD), BF16)
    a_dif = jnp.concatenate([jnp.concatenate([jnp.where(lane64 < ROPE_D, dq8, zq), z8], axis=1),
                             jnp.concatenate([jnp.where(lane64 < ROPE_D, zq, dq8), z8], axis=1)], axis=0)
    sub = lax.broadcasted_iota(jnp.int32, (HEADS, GROUP), 0)
    blk = lax.broadcasted_iota(jnp.int32, (HEADS, GROUP), 1) // HD
    diag = sub == blk
    a_pe = q8m[:, HD:HD + ROPE_D]
    ident_mask = (lax.broadcasted_iota(jnp.int32, (HEADS, HEADS), 0)
                  == lax.broadcasted_iota(jnp.int32, (HEADS, HEADS), 1))

    @pl.when(c_id == 0)
    def _():
        ksn = ksn_ref[0].astype(F32)
        qf = q8m.astype(F32)
        s_m = jnp.sum(qf * kn8_ref[0].astype(F32), axis=-1, keepdims=True)
        kf = ksn[:, LANES:2 * LANES]
        s_f = jnp.sum(fq8.astype(F32) * kf[:, 0:HD], axis=-1, keepdims=True)
        kd = ksn[:, 5 * LANES:6 * LANES]
        s_d = jnp.sum(a_dif.astype(F32) * kd, axis=-1, keepdims=True)
        for (m_ref, l_ref, acc_ref), s0, v0 in ((sm_a, s_m, ksn[:, 0:LANES]),
                                                (sm_f, s_f, ksn[:, 2 * LANES:2 * LANES + HD]),
                                                (sm_d, s_d, kd)):
            m_ref[...] = s0
            l_ref[...] = jnp.ones_like(l_ref)
            acc_ref[...] = jnp.broadcast_to(v0, acc_ref.shape)
        acc_s[...] = jnp.zeros_like(acc_s)
        r_s[...] = jnp.zeros_like(r_s)
        aft_f[...] = jnp.sum(jnp.where(ident_mask, lfn_ref[0], 0.0), axis=-1, keepdims=True)

    wukt = wukt_ref[...]
    rr = lax.broadcasted_iota(jnp.int32, (LANES, LANES), 0)
    cc = lax.broadcasted_iota(jnp.int32, (LANES, LANES), 1)
    t_after = (rr > cc).astype(BF16)

    def sm_weights(st, s):
        m_ref, l_ref, _ = st
        m = m_ref[...]
        m_new = jnp.maximum(m, jnp.max(s, axis=-1, keepdims=True))
        alpha = jnp.exp(m - m_new)
        e = jnp.exp(s - m_new)
        l_ref[...] = alpha * l_ref[...] + jnp.sum(e, axis=-1, keepdims=True)
        m_ref[...] = m_new
        return alpha, e.astype(BF16)

    fkt = b_fox[slot, 0:HD, :].astype(BF16)
    fvt = b_fox[slot, HD:2 * HD, :].astype(BF16)
    skv = b_sb[slot].astype(BF16)
    dkv = b_dif[slot].astype(BF16)
    f_later, f_total = _suffix_excl(b_fox[slot, 2 * HD:2 * HD + HEADS, :], t_after)
    s_fox = _dot(fq8, fkt) + f_later + aft_f[...]
    z = _dot_nt(a_sb, skv)
    s_dif = _dot_nt(a_dif, dkv)
    lsp = _log_sigmoid(z)
    s_later, s_total = _suffix_excl(lsp - z, t_after)
    ckvt = b_mla[slot, 0:MLA_KV_RANK, :].astype(BF16)
    kpet = b_mla[slot, MLA_KV_RANK:MLA_KV_RANK + ROPE_D, :].astype(BF16)
    qcol = jnp.broadcast_to(qcol_ref[0], (GROUP, LANES))
    s_parts = []
    for j in range(ck // kb):
        knt = _dot(wukt, ckvt[:, j * kb:(j + 1) * kb])
        qk, n2 = [], []
        for hh in range(HEADS):
            kh = knt[hh * HD:(hh + 1) * HD, :]
            qk.append(jnp.sum(kh * _tile_lanes(qcol[hh * HD:(hh + 1) * HD, :], kb // LANES), axis=0, keepdims=True))
            n2.append(jnp.sum(kh * kh, axis=0, keepdims=True))
        inv = lax.rsqrt(jnp.concatenate(n2, axis=0) * (1.0 / HD) + EPS)
        s_parts.append(jnp.concatenate(qk, axis=0) * inv)
    s_mla = jnp.concatenate(s_parts, axis=1) + _dot(a_pe, kpet)
    al_f, e_f = sm_weights(sm_f, s_fox)
    al_d, e_d = sm_weights(sm_d, s_dif)
    w_s = jnp.exp(lsp + s_later + r_s[...]).astype(BF16)
    al_a, e_a = sm_weights(sm_a, s_mla)
    acc_f[...] = al_f * acc_f[...] + _dot_nt(e_f, fvt)
    acc_d[...] = al_d * acc_d[...] + _dot(e_d, dkv)
    acc_s[...] = acc_s[...] + _dot(w_s, skv)
    acc_a[...] = al_a * acc_a[...] + _dot_nt(e_a, ckvt)
    aft_f[...] = aft_f[...] + f_total
    r_s[...] = r_s[...] + s_total

    @pl.when(c_id == nch - 1)
    def _():
        olat = (sm_a[2][...] / sm_a[1][...]).astype(BF16)
        r = _dot(olat, wuv_ref[...])
        oa_ref[0] = jnp.sum(jnp.where(diag, r, 0.0), axis=0, keepdims=True)
        ob_ref[0] = sm_f[2][...] / sm_f[1][...]
        oc_ref[0] = acc_s[...]
        lam = _diff_lambda(lam_ref, lam_init)
        dn = sm_d[2][...] / sm_d[1][...]
        od = dn[0:HEADS] - lam * dn[HEADS:2 * HEADS]
        ss = jnp.sum(jnp.where(lo, 0.0, od * od), axis=-1, keepdims=True)
        od_ref[0] = od * lax.rsqrt(ss * (1.0 / HD) + EPS) * dg_ref[...] * (1.0 - lam_init)


def _attn_sample(page_table, q8m, qcol, fq8, sq8, dq8, kn8, ksn, lfn, wukt, wuv, dg, lamv, caches, layer, ppc, kb,
                 lam_init):
    ns = q8m.shape[0]
    n_pages = page_table.shape[1]
    page = caches[2].shape[2]
    nch = n_pages // ppc
    ck = ppc * page
    seq3 = lambda a: pl.BlockSpec((1,) + a.shape[1:], lambda s, c, pt: (s, 0, 0))
    const = lambda a: pl.BlockSpec(a.shape, lambda s, c, pt: (0,) * a.ndim)
    anyspec = pl.BlockSpec(memory_space=pl.ANY)
    out8 = lambda w: pl.BlockSpec((1, HEADS, w), lambda s, c, pt: (s, 0, 0))
    grid_spec = pltpu.PrefetchScalarGridSpec(
        num_scalar_prefetch=1,
        grid=(ns, nch),
        in_specs=[seq3(q8m), seq3(qcol), seq3(fq8), seq3(sq8), seq3(dq8), seq3(kn8), seq3(ksn), seq3(lfn),
                  const(wukt), const(wuv), const(dg), const(lamv), anyspec, anyspec, anyspec, anyspec],
        out_specs=(pl.BlockSpec((1, 1, GROUP), lambda s, c, pt: (s, 0, 0)), out8(HD), out8(LANES), out8(LANES)),
        scratch_shapes=[
            pltpu.VMEM((2, caches[0].shape[2], ck), F32),
            pltpu.VMEM((2, caches[1].shape[2], ck), F32),
            pltpu.VMEM((2, ck, caches[2].shape[3]), F32),
            pltpu.VMEM((2, ck, caches[3].shape[3]), F32),
            pltpu.SemaphoreType.DMA((2, 4)),
            pltpu.VMEM((HEADS, 1), F32), pltpu.VMEM((HEADS, 1), F32), pltpu.VMEM((HEADS, LANES), F32),
            pltpu.VMEM((HEADS, 1), F32), pltpu.VMEM((HEADS, 1), F32), pltpu.VMEM((HEADS, HD), F32),
            pltpu.VMEM((HEADS, 1), F32),
            pltpu.VMEM((HEADS, 1), F32), pltpu.VMEM((HEADS, LANES), F32),
            pltpu.VMEM((2 * HEADS, 1), F32), pltpu.VMEM((2 * HEADS, 1), F32), pltpu.VMEM((2 * HEADS, LANES), F32),
        ],
    )
    return pl.pallas_call(
        functools.partial(_attn_sample_kernel, layer=layer, ppc=ppc, kb=kb, page=page, lam_init=lam_init),
        out_shape=(jax.ShapeDtypeStruct((ns, 1, GROUP), F32), jax.ShapeDtypeStruct((ns, HEADS, HD), F32),
                   jax.ShapeDtypeStruct((ns, HEADS, LANES), F32), jax.ShapeDtypeStruct((ns, HEADS, LANES), F32)),
        grid_spec=grid_spec,
        compiler_params=_cparams(("arbitrary", "arbitrary")),
        name="attn_sample",
    )(page_table, q8m, qcol, fq8, sq8, dq8, kn8, ksn, lfn, wukt, wuv, dg, lamv, *caches)


def _group_matrices():
    def pair(groups):
        w = len(groups)
        e = np.zeros((w, LANES), np.float32)
        for i, g in enumerate(groups):
            if g >= 0:
                e[i, g] = 1.0
        return jnp.asarray(e, BF16), jnp.asarray(e.T.copy(), BF16)
    q_groups = [2 * (i // LANES) + (0 if i % LANES < HD else 1) if i % LANES < HD + ROPE_D else -1
                for i in range(HEADS * LANES)]
    k_groups = [i // LANES if i % LANES < HD else -1 for i in range(HEADS * LANES)]
    g64 = [i // HD for i in range(GROUP)]
    g32 = [i // ROPE_D for i in range(GROUP)]
    out = []
    for g in (q_groups, k_groups, g64, g32):
        out.extend(pair(g))
    return tuple(out)


def _rope_tables(pos):
    half = ROPE_D // 2
    inv = ROPE_THETA ** (-jnp.arange(half, dtype=F32) * (2.0 / ROPE_D))
    ang = pos.astype(F32)[:, None] * inv[None, :]
    cos, sin = jnp.cos(ang), jnp.sin(ang)
    n = pos.shape[0]
    cos32 = jnp.concatenate([cos, cos], axis=1)
    sin32 = jnp.concatenate([-sin, sin], axis=1)
    c_m = jnp.concatenate([jnp.ones((n, HD), F32), cos32, jnp.ones((n, LANES - HD - ROPE_D), F32)], axis=1)
    s_m = jnp.concatenate([jnp.zeros((n, HD), F32), sin32, jnp.zeros((n, LANES - HD - ROPE_D), F32)], axis=1)
    c_d = jnp.tile(cos32, (1, LANES // ROPE_D))
    s_d = jnp.tile(sin32, (1, LANES // ROPE_D))
    return jnp.concatenate([c_m, s_m, c_d, s_d], axis=1)


def _pad_cols(a, w):
    return jnp.pad(a, [(0, 0)] * (a.ndim - 1) + [(0, w - a.shape[-1])])


def _gain_table(P):
    L = P["mla_cq_norm"].shape[0]
    ones32 = jnp.ones((L, ROPE_D), F32)
    zeros32 = jnp.zeros((L, ROPE_D), F32)
    rows = [None] * GV_ROWS
    rows[G_CQ] = P["mla_cq_norm"]
    rows[G_CKV] = P["mla_ckv_norm"]
    rows[G_Q1] = jnp.tile(jnp.concatenate([P["mla_qn_gain"], P["mla_qr_gain"], zeros32], axis=1), (1, HEADS))
    rows[G_Q2] = jnp.tile(jnp.concatenate([P["mla_kn_gain"], ones32, zeros32], axis=1), (1, HEADS))
    rows[G_KR] = P["mla_kr_gain"]
    rows[G_FQ] = jnp.tile(P["fox_q_gain"], (1, HEADS))
    rows[G_FK] = P["fox_k_gain"]
    rows[G_FB] = P["fox_f_bias"]
    rows[G_DQ] = jnp.tile(P["diff_q_gain"], (1, 2 * HEADS))
    rows[G_DK] = jnp.tile(P["diff_k_gain"], (1, 2))
    cnt = jnp.tile(jnp.asarray([[1.0 / HD, 1.0 / ROPE_D]], F32), (L, HEADS))
    rows[G_QCNT] = jnp.concatenate([cnt, jnp.ones((L, LANES - 2 * HEADS), F32)], axis=1)
    rows = [jnp.zeros((L, GV_W), F32) if r is None else _pad_cols(r.astype(F32), GV_W) for r in rows]
    return jnp.stack(rows, axis=1)


def _win_padded(w_in):
    parts, start = [], 0
    for s in IN_SIZES:
        parts.append(w_in[..., start:start + s])
        start += s
    cq, ckv, kpe, fq, fk, fv, fz, sq, sk, sv, dq, dk, dv = parts
    cols = [cq, ckv, _pad_cols(kpe, LANES), fq, fk, fv, _pad_cols(fz, LANES), sq, sk, sv, dq, dk, dv]
    return jnp.concatenate(cols, axis=-1).astype(BF16)


def kernel(x_prompt, x_sample, mem_prompt, cache_mla, cache_fox, cache_sb, cache_diff, cache_mem, page_table, ffn1_norm, ffn1_w1, ffn1_w3, ffn1_w2, mix_norm, w_in, fox_f_bias, mla_cq_norm, mla_ckv_norm, mla_w_uq, mla_w_uk, mla_w_uv, mla_qn_gain, mla_qr_gain, mla_kn_gain, mla_kr_gain, fox_q_gain, fox_k_gain, diff_q_gain, diff_k_gain, diff_lam_q1, diff_lam_k1, diff_lam_q2, diff_lam_k2, diff_sub_gain, w_out, mem_q_norm, mem_kv_norm, mem_w_q, mem_w_k, mem_w_v, mem_q_gain, mem_k_gain, mem_w_o, ffn2_norm, ffn2_w1, ffn2_w3, ffn2_w2):
    batch, seq, d = x_prompt.shape
    ns = x_sample.shape[0]
    depth = w_in.shape[0]
    n_mem = mem_prompt.shape[1]
    n_pages = page_table.shape[1]
    page = cache_mla.shape[2]
    past = n_pages * page
    assert x_sample.shape[1] == 1
    mp = batch * seq

    tm_ffn = min(512, seq)
    tf = 512
    tm = min(256, seq)
    tq = min(512, seq)
    ppc = min(32, n_pages)
    kb = min(512, ppc * page)
    gsz = 8

    bf = lambda a: a.astype(BF16)
    f1 = (bf(ffn1_w1), bf(ffn1_w3), bf(ffn1_w2))
    f2 = (bf(ffn2_w1), bf(ffn2_w3), bf(ffn2_w2))
    win_p = _win_padded(w_in)
    wuq_p = bf(_pad_cols(mla_w_uq.reshape(depth, MLA_Q_RANK, HEADS, HD + ROPE_D), LANES)
               .reshape(depth, MLA_Q_RANK, HEADS * LANES))
    wuk_p = bf(_pad_cols(mla_w_uk, LANES).reshape(depth, MLA_KV_RANK, HEADS * LANES))
    wukt = bf(jnp.transpose(mla_w_uk.reshape(depth, MLA_KV_RANK, GROUP), (0, 2, 1)))
    wuv = bf(mla_w_uv.reshape(depth, MLA_KV_RANK, GROUP))
    wout = bf(w_out)
    wq, wk, wv, wo = bf(mem_w_q), bf(mem_w_k), bf(mem_w_v), bf(mem_w_o)
    P = dict(mla_cq_norm=mla_cq_norm, mla_ckv_norm=mla_ckv_norm, mla_qn_gain=mla_qn_gain, mla_qr_gain=mla_qr_gain,
             mla_kn_gain=mla_kn_gain, mla_kr_gain=mla_kr_gain, fox_q_gain=fox_q_gain, fox_k_gain=fox_k_gain,
             fox_f_bias=fox_f_bias, diff_q_gain=diff_q_gain, diff_k_gain=diff_k_gain)
    gv = _gain_table(P)
    emats = _group_matrices()
    lamv = _pad_cols(jnp.stack([diff_lam_q1, diff_lam_k1, diff_lam_q2, diff_lam_k2], axis=1).astype(F32), LANES)
    dg_pair = jnp.tile(diff_sub_gain.astype(F32), (1, 2))[:, None, :]
    dg_hi = jnp.concatenate([jnp.zeros((depth, 1, HD), F32), diff_sub_gain.astype(F32)[:, None, :]], axis=2)
    row1 = lambda a, l: a[l][None, :].astype(F32)

    tab_p = _rope_tables(jnp.tile(jnp.arange(seq, dtype=jnp.int32), batch))
    tab_s = _rope_tables(jnp.full((ns,), past, jnp.int32))
    caches = (jnp.transpose(cache_mla, (0, 1, 3, 2)), jnp.transpose(cache_fox, (0, 1, 3, 2)), cache_sb, cache_diff)

    xp = x_prompt.reshape(mp, d)
    xs = x_sample.reshape(ns, d)
    memf = mem_prompt.reshape(batch * n_mem, d)
    outs_p = [[] for _ in range(5)]
    outs_s = [[] for _ in range(4)]
    for l in range(depth):
        lam_init = 0.8 - 0.6 * math.exp(-0.3 * l)
        mkv = _memkv(memf, row1(mem_kv_norm, l), wk[l], wv[l], row1(mem_k_gain, l), min(256, n_mem))
        xp = _ffn(xp, row1(ffn1_norm, l), f1[0][l], f1[1][l], f1[2][l], tm_ffn, tf)
        qmla, qfsd, kmla, ksh, r_mla, r_fox, r_sb, r_dif = _proj(
            xp, row1(mix_norm, l), win_p[l], wuq_p[l], wuk_p[l], gv[l], tab_p, emats, tm)
        logf = r_fox[:, 2 * HD:2 * HD + HEADS].reshape(batch, seq, HEADS)
        fcc, fcr = _fc(_pad_cols(logf, LANES), jnp.transpose(logf, (0, 2, 1)), tq)
        o4 = _attn_prompt(qmla, qfsd, kmla, ksh, fcc, fcr, wuv[l], dg_pair[l], lamv[l], batch, seq, tq, lam_init)
        x1, qm = _out_memq(xp, o4, wout[l], row1(mem_q_norm, l), wq[l], row1(mem_q_gain, l), tm)
        x2 = _mem_prompt(x1, qm, mkv, wo[l], batch, seq, n_mem, tm)
        xp = _ffn(x2, row1(ffn2_norm, l), f2[0][l], f2[1][l], f2[2][l], tm_ffn, tf)
        for lst, r in zip(outs_p, (r_mla, r_fox, r_sb, r_dif, mkv)):
            lst.append(r)
        xs = _ffn(xs, row1(ffn1_norm, l), f1[0][l], f1[1][l], f1[2][l], ns, tf)
        qmla, qfsd, kmla, ksh, r_mla, r_fox, r_sb, r_dif = _proj(
            xs, row1(mix_norm, l), win_p[l], wuq_p[l], wuk_p[l], gv[l], tab_s, emats, ns)
        q8m = qmla.reshape(ns, HEADS, LANES)
        fq8 = qfsd[:, 0:GROUP].reshape(ns, HEADS, HD)
        sq8 = qfsd[:, GROUP:2 * GROUP].reshape(ns, HEADS, HD)
        dq8 = qfsd[:, 2 * GROUP:3 * GROUP].reshape(ns, HEADS, HD)
        oa, ob, oc, od = _attn_sample(
            page_table, q8m, q8m[:, :, 0:HD].astype(F32).reshape(ns, GROUP, 1), fq8, sq8, dq8,
            kmla.reshape(ns, HEADS, LANES), ksh.reshape(ns, 1, KSH_W),
            r_fox[:, 2 * HD:2 * HD + HEADS].reshape(ns, 1, HEADS), wukt[l], wuv[l], dg_hi[l], lamv[l],
            caches, l, ppc, kb, lam_init)
        hi = lambda a: a[:, :, HD:].reshape(ns, GROUP).astype(BF16)
        o4 = (oa.reshape(ns, GROUP).astype(BF16), ob.reshape(ns, GROUP).astype(BF16), hi(oc), hi(od))
        x1, qm = _out_memq(xs, o4, wout[l], row1(mem_q_norm, l), wq[l], row1(mem_q_gain, l), ns)
        x2 = _mem_sample(x1, qm.astype(F32).reshape(ns, MEM_HEADS, MEM_HD), cache_mem, l, wo[l], gsz)
        xs = _ffn(x2, row1(ffn2_norm, l), f2[0][l], f2[1][l], f2[2][l], ns, tf)
        for lst, r in zip(outs_s, (r_mla, r_fox, r_sb, r_dif)):
            lst.append(r)

    stack_p = lambda lst: jnp.stack(lst).reshape(depth, batch, seq, -1)
    stack_s = lambda lst: jnp.stack(lst).reshape(depth, ns, 1, -1)
    return (xp.reshape(batch, seq, d), xs.reshape(ns, 1, d),
            stack_p(outs_p[0]), stack_p(outs_p[1]), stack_p(outs_p[2]), stack_p(outs_p[3]),
            jnp.stack(outs_p[4]).reshape(depth, batch, n_mem, 2, MEM_HEADS, MEM_HD),
            stack_s(outs_s[0]), stack_s(outs_s[1]), stack_s(outs_s[2]), stack_s(outs_s[3]))
```

```python
import functools
import math

import numpy as np
import jax
import jax.numpy as jnp
from jax import lax
from jax.experimental import pallas as pl
from jax.experimental.pallas import tpu as pltpu

F32 = jnp.float32
BF16 = jnp.bfloat16
EPS = 1e-6
ROPE_THETA = 10000.0
NEG_INF = -1e30

LANES = 128
VMEM_LIMIT_BYTES = 56 * 1024 * 1024

HEADS = 8
HD = 64
ROPE_D = 32
MLA_Q_RANK = 384
MLA_KV_RANK = 128
MEM_HEADS = 4
MEM_HD = 128
GROUP = HEADS * HD

MLA_SCALE = (HD + ROPE_D) ** -0.5
FOX_SCALE = HD ** -0.5
SB_SCALE = HD ** -0.5
DIFF_SCALE = ROPE_D ** -0.5
MEM_SCALE = MEM_HD ** -0.5

C_CQ, C_CKV, C_KPE, C_FQ, C_FKV, C_FZ, C_SQ, C_SKV, C_DQ, C_DKV, C_END = (
    0, 384, 512, 640, 1152, 1280, 1408, 1920, 2048, 2560, 2688)
IN_SIZES = (384, 128, 32, 512, 64, 64, 8, 512, 64, 64, 512, 64, 64)

(G_CQ, G_CKV, G_Q1, G_Q2, G_KR, G_FQ, G_FK, G_FB, G_DQ, G_DK, G_QCNT) = range(11)
GV_ROWS = 16
GV_W = 1024


def _cparams(sem):
    return pltpu.CompilerParams(dimension_semantics=sem, vmem_limit_bytes=VMEM_LIMIT_BYTES)


def _dot(a, b):
    return jnp.dot(a, b, preferred_element_type=F32)


def _dot_nt(a, b):
    return lax.dot_general(a, b, (((1,), (1,)), ((), ())), preferred_element_type=F32)


def _rms_rows(x, g):
    return x * lax.rsqrt(jnp.mean(x * x, axis=-1, keepdims=True) + EPS) * g


def _split2(x):
    hi = x.astype(BF16)
    lo = (x - hi.astype(F32)).astype(BF16)
    return hi, lo


def _split3(x):
    h1 = x.astype(BF16)
    r = x - h1.astype(F32)
    h2 = r.astype(BF16)
    h3 = (r - h2.astype(F32)).astype(BF16)
    return h1, h2, h3


def _seg_inv_rms(x, e, et, inv_cnt):
    hi, lo = _split2(x * x)
    ss = _dot(hi, e) + _dot(lo, e)
    inv = lax.rsqrt(ss * inv_cnt + EPS)
    ihi, ilo = _split2(inv)
    return _dot(ihi, et) + _dot(ilo, et)


def _log_sigmoid(z):
    return jnp.minimum(z, 0.0) - jnp.log1p(jnp.exp(-jnp.abs(z)))


def _swap_halves(x):
    w = x.shape[1]
    lane = lax.broadcasted_iota(jnp.int32, x.shape, 1)
    return jnp.where((lane % ROPE_D) < (ROPE_D // 2), pltpu.roll(x, w - ROPE_D // 2, 1),
                     pltpu.roll(x, ROPE_D // 2, 1))


def _tile_lanes(t, n):
    return jnp.concatenate([t] * n, axis=1)


def _ffn_kernel(x_ref, g_ref, w1_ref, w3_ref, w2_ref, o_ref, h_ref, acc_ref):
    f = pl.program_id(1)

    @pl.when(f == 0)
    def _():
        h_ref[...] = _rms_rows(x_ref[...], g_ref[...]).astype(BF16)
        acc_ref[...] = jnp.zeros_like(acc_ref)

    h = h_ref[...]
    a = _dot(h, w1_ref[...])
    b = _dot(h, w3_ref[...])
    act = (a * jax.nn.sigmoid(a)) * b
    acc_ref[...] += _dot(act.astype(BF16), w2_ref[...])

    @pl.when(f == pl.num_programs(1) - 1)
    def _():
        o_ref[...] = x_ref[...] + 0.5 * acc_ref[...]


def _ffn(x, g, w1, w3, w2, layer, tm, tf):
    m, d = x.shape
    dff = w1.shape[2]
    return pl.pallas_call(
        _ffn_kernel,
        out_shape=jax.ShapeDtypeStruct((m, d), F32),
        grid=(m // tm, dff // tf),
        in_specs=[
            pl.BlockSpec((tm, d), lambda i, f: (i, 0)),
            pl.BlockSpec((1, d), lambda i, f: (0, 0)),
            pl.BlockSpec((None, d, tf), lambda i, f: (layer, 0, f)),
            pl.BlockSpec((None, d, tf), lambda i, f: (layer, 0, f)),
            pl.BlockSpec((None, tf, d), lambda i, f: (layer, f, 0)),
        ],
        out_specs=pl.BlockSpec((tm, d), lambda i, f: (i, 0)),
        scratch_shapes=[pltpu.VMEM((tm, d), BF16), pltpu.VMEM((tm, d), F32)],
        compiler_params=_cparams(("parallel", "arbitrary")),
        name="ffn",
    )(x, g, w1, w3, w2)


def _proj_kernel(x_ref, g_ref, win_ref, wuq_ref, wuk_ref, gv_ref, tab_ref,
                 eq_ref, etq_ref, ek_ref, etk_ref, e64_ref, et64_ref, e32_ref, et32_ref,
                 qmla_ref, qfsd_ref, kmla_ref, ksh_ref, mla_row_ref, fox_row_ref, sb_row_ref, diff_row_ref):
    h = _rms_rows(x_ref[...], g_ref[...]).astype(BF16)
    u = _dot(h, win_ref[...])
    tab = tab_ref[...]
    c_m, s_m = tab[:, 0:LANES], tab[:, LANES:2 * LANES]
    c_d, s_d = tab[:, 2 * LANES:3 * LANES], tab[:, 3 * LANES:4 * LANES]
    lane = lax.broadcasted_iota(jnp.int32, (u.shape[0], LANES), 1)
    lo64 = lane < HD

    def rope(x, c, s):
        n = x.shape[1] // LANES
        return x * _tile_lanes(c, n) + _swap_halves(x) * _tile_lanes(s, n)

    cqn = _rms_rows(u[:, C_CQ:C_CKV], gv_ref[G_CQ:G_CQ + 1, 0:MLA_Q_RANK]).astype(BF16)
    q = _dot(cqn, wuq_ref[...])
    inv = _seg_inv_rms(q, eq_ref[...], etq_ref[...], gv_ref[G_QCNT:G_QCNT + 1, 0:LANES])
    qn = q * inv * gv_ref[G_Q1:G_Q1 + 1, :] * gv_ref[G_Q2:G_Q2 + 1, :]
    qmla_ref[...] = (rope(qn, c_m, s_m) * MLA_SCALE).astype(BF16)

    ckvn = _rms_rows(u[:, C_CKV:C_KPE], gv_ref[G_CKV:G_CKV + 1, 0:LANES])
    kp = u[:, C_KPE:C_FQ]
    kpn = kp * lax.rsqrt(jnp.sum(kp * kp, axis=-1, keepdims=True) * (1.0 / ROPE_D) + EPS) \
        * gv_ref[G_KR:G_KR + 1, 0:LANES]
    kpr = rope(kpn, c_d, s_d)
    mla_row_ref[:, 0:LANES] = ckvn
    mla_row_ref[:, LANES:LANES + ROPE_D] = kpr[:, 0:ROPE_D]
    ckvb = ckvn.astype(BF16)
    kn = _dot(ckvb, wuk_ref[...])
    kns = kn * _seg_inv_rms(kn, ek_ref[...], etk_ref[...], 1.0 / HD)
    kmla_ref[...] = (kns + _tile_lanes(pltpu.roll(kpr, HD, 1), HEADS)).astype(BF16)

    fq = u[:, C_FQ:C_FKV]
    fqn = fq * _seg_inv_rms(fq, e64_ref[...], et64_ref[...], 1.0 / HD) * gv_ref[G_FQ:G_FQ + 1, 0:GROUP]
    fkv = u[:, C_FKV:C_FZ]
    ssf = jnp.sum(jnp.where(lo64, fkv * fkv, 0.0), axis=-1, keepdims=True)
    fkvn = jnp.where(lo64, fkv * lax.rsqrt(ssf * (1.0 / HD) + EPS) * gv_ref[G_FK:G_FK + 1, 0:LANES], fkv)
    logf = _log_sigmoid(u[:, C_FZ:C_SQ] + gv_ref[G_FB:G_FB + 1, 0:LANES])
    fox_row_ref[:, 0:LANES] = fkvn
    fox_row_ref[:, LANES:LANES + HEADS] = logf[:, 0:HEADS]

    skv = u[:, C_SKV:C_DQ]
    sb_row_ref[...] = skv

    dq = u[:, C_DQ:C_DKV]
    dqn = dq * _seg_inv_rms(dq, e32_ref[...], et32_ref[...], 1.0 / ROPE_D) * gv_ref[G_DQ:G_DQ + 1, 0:GROUP]
    dkv = u[:, C_DKV:C_END]
    d2 = dkv * dkv
    ss0 = jnp.sum(jnp.where(lane < ROPE_D, d2, 0.0), axis=-1, keepdims=True)
    ss1 = jnp.sum(jnp.where(jnp.logical_and(lane >= ROPE_D, lo64), d2, 0.0), axis=-1, keepdims=True)
    invd = jnp.where(lane < ROPE_D, lax.rsqrt(ss0 * (1.0 / ROPE_D) + EPS), lax.rsqrt(ss1 * (1.0 / ROPE_D) + EPS))
    dkr = rope(dkv * invd * gv_ref[G_DK:G_DK + 1, 0:LANES], c_d, s_d)
    dkvn = jnp.where(lo64, dkr, dkv)
    diff_row_ref[...] = dkvn

    qfsd_ref[:, 0:GROUP] = (fqn * FOX_SCALE).astype(BF16)
    qfsd_ref[:, GROUP:2 * GROUP] = (u[:, C_SQ:C_SKV] * SB_SCALE).astype(BF16)
    qfsd_ref[:, 2 * GROUP:3 * GROUP] = (rope(dqn, c_d, s_d) * DIFF_SCALE).astype(BF16)

    ksh_ref[:, 0:LANES] = ckvb
    for k, v in enumerate((fkvn, skv, dkvn)):
        ksh_ref[:, (1 + 2 * k) * LANES:(2 + 2 * k) * LANES] = v.astype(BF16)
        ksh_ref[:, (2 + 2 * k) * LANES:(3 + 2 * k) * LANES] = pltpu.roll(v, HD, 1).astype(BF16)


KSH_W = 7 * LANES


def _proj(x, g, win, wuq, wuk, gv, tab, emats, tm):
    m, d = x.shape
    const = lambda a: pl.BlockSpec(a.shape, lambda i: (0,) * a.ndim)
    row = lambda w: pl.BlockSpec((tm, w), lambda i: (i, 0))
    out_w = (8 * LANES, 3 * GROUP, 8 * LANES, KSH_W, 160, 136, 128, 128)
    out_dt = (BF16, BF16, BF16, BF16, F32, F32, F32, F32)
    return pl.pallas_call(
        _proj_kernel,
        out_shape=tuple(jax.ShapeDtypeStruct((m, w), t) for w, t in zip(out_w, out_dt)),
        grid=(m // tm,),
        in_specs=[row(d), const(g), const(win), const(wuq), const(wuk), const(gv), row(4 * LANES)]
                 + [const(e) for e in emats],
        out_specs=tuple(row(w) for w in out_w),
        compiler_params=_cparams(("parallel",)),
        name="mixer_proj",
    )(x, g, win, wuq, wuk, gv, tab, *emats)


def _fc_kernel(lc_ref, lr_ref, fcc_ref, fcr_ref, *, tb):
    s = lc_ref.shape[1]
    r = lax.broadcasted_iota(jnp.int32, (tb, tb), 0)
    c = lax.broadcasted_iota(jnp.int32, (tb, tb), 1)
    tril = (c <= r).astype(BF16)
    triu = (r <= c).astype(BF16)
    carry_c = jnp.zeros((1, LANES), F32)
    carry_r = jnp.zeros((HEADS, 1), F32)
    for j in range(s // tb):
        xc = lc_ref[0, j * tb:(j + 1) * tb, :]
        cs = sum(_dot(tril, p) for p in _split3(xc)) + carry_c
        fcc_ref[0, j * tb:(j + 1) * tb, :] = cs
        carry_c = cs[tb - 1:tb, :]
        xr = lr_ref[0, :, j * tb:(j + 1) * tb]
        csr = sum(_dot(p, triu) for p in _split3(xr)) + carry_r
        fcr_ref[0, j] = csr
        carry_r = csr[:, tb - 1:tb]


def _fc(logf_col, logf_row, tb):
    b, s, _ = logf_col.shape
    return pl.pallas_call(
        functools.partial(_fc_kernel, tb=tb),
        out_shape=(jax.ShapeDtypeStruct((b, s, LANES), F32), jax.ShapeDtypeStruct((b, s // tb, HEADS, tb), F32)),
        grid=(b,),
        in_specs=[pl.BlockSpec((1, s, LANES), lambda i: (i, 0, 0)), pl.BlockSpec((1, HEADS, s), lambda i: (i, 0, 0))],
        out_specs=(pl.BlockSpec((1, s, LANES), lambda i: (i, 0, 0)),
                   pl.BlockSpec((1, s // tb, HEADS, tb), lambda i: (i, 0, 0, 0))),
        compiler_params=_cparams(("parallel",)),
        name="forget_prefix",
    )(logf_col, logf_row)


def _diff_lambda(lam_ref, lam_init):
    a = jnp.sum(lam_ref[0:1, :] * lam_ref[1:2, :], axis=-1, keepdims=True)
    b = jnp.sum(lam_ref[2:3, :] * lam_ref[3:4, :], axis=-1, keepdims=True)
    return jnp.exp(a) - jnp.exp(b) + lam_init


def _attn_prompt_kernel(qm_ref, fq_ref, sq_ref, dq_ref, km_ref, ks_ref, fcc_ref, fcr_ref, wuv_ref, dg_ref, lam_ref,
                        oa_ref, ob_ref, oc_ref, od_ref, *, tq, tk, lam_init):
    p = pl.program_id(1)
    i = pl.program_id(2)
    lane = lax.broadcasted_iota(jnp.int32, (tq, LANES), 1)
    lo = lane < HD
    fqp, sqp, dqp = (r[...].astype(F32) for r in (fq_ref, sq_ref, dq_ref))
    q_mla = (qm_ref[:, 0:LANES], qm_ref[:, LANES:2 * LANES])
    q_fox = (jnp.where(lo, fqp, 0.0).astype(BF16), jnp.where(lo, 0.0, fqp).astype(BF16))
    q_sb = (jnp.where(lo, sqp, 0.0).astype(BF16), jnp.where(lo, 0.0, sqp).astype(BF16))
    q_dif = tuple(jnp.where(jnp.logical_and(lane >= ROPE_D * c, lane < ROPE_D * (c + 1)), dqp, 0.0).astype(BF16)
                  for c in range(4))

    fcc = fcc_ref[0]
    fcq = tuple(jnp.sum(jnp.where(lane == 2 * p + r, fcc, 0.0), axis=-1, keepdims=True) for r in range(2))

    rr = lax.broadcasted_iota(jnp.int32, (tk, tk), 0)
    cc = lax.broadcasted_iota(jnp.int32, (tk, tk), 1)
    t_after = (rr > cc).astype(BF16)

    def softmax_update(st, q, k, v, bias, mask):
        m, l, acc = st
        s = _dot_nt(q, k)
        if bias is not None:
            s = s + bias
        if mask is not None:
            s = jnp.where(mask, s, NEG_INF)
        m_new = jnp.maximum(m, jnp.max(s, axis=-1, keepdims=True))
        alpha = jnp.exp(m - m_new)
        e = jnp.exp(s - m_new)
        l = alpha * l + jnp.sum(e, axis=-1, keepdims=True)
        acc = alpha * acc + _dot(e.astype(BF16), v)
        return m_new, l, acc

    def sb_update(st, q, k, v, strict):
        r_sum, acc = st
        z = _dot_nt(q, k)
        lsp = _log_sigmoid(z)
        lsn = lsp - z
        if strict is not None:
            lsn = jnp.where(strict, lsn, 0.0)
        hi, lo2 = _split2(lsn)
        cs = _dot(hi, t_after) + _dot(lo2, t_after)
        w = jnp.exp(lsp + cs + r_sum)
        if strict is not None:
            w = jnp.where(strict, w, 0.0)
        acc = acc + _dot(w.astype(BF16), v)
        return r_sum + jnp.sum(lsn, axis=-1, keepdims=True), acc

    def step(j, carry, masked):
        mla, fox, sb, dif = carry
        ks0 = pl.multiple_of(j * tk, tk)
        kmb = km_ref[pl.ds(ks0, tk), :]
        ksb = ks_ref[pl.ds(ks0, tk), :]
        seg = lambda n: ksb[:, n * LANES:(n + 1) * LANES]
        ckv, kf, kfs, ksk, kss, kd, kds = (seg(n) for n in range(7))
        causal = strict = None
        if masked:
            row = lax.broadcasted_iota(jnp.int32, (tq, tk), 0)
            col = lax.broadcasted_iota(jnp.int32, (tq, tk), 1)
            causal, strict = col <= row, col < row
        mla = tuple(softmax_update(mla[r], q_mla[r], kmb[:, r * LANES:(r + 1) * LANES], ckv, None, causal)
                    for r in range(2))
        fck = tuple(fcr_ref[0, j, pl.ds(2 * p + r, 1), :] for r in range(2))
        fox = (softmax_update(fox[0], q_fox[0], kf, kfs, fcq[0] - fck[0], causal),
               softmax_update(fox[1], q_fox[1], kfs, kf, fcq[1] - fck[1], causal))
        sb = (sb_update(sb[0], q_sb[0], ksk, kss, strict), sb_update(sb[1], q_sb[1], kss, ksk, strict))
        dif = (softmax_update(dif[0], q_dif[0], kd, kds, None, causal),
               softmax_update(dif[1], q_dif[1], kd, kds, None, causal),
               softmax_update(dif[2], q_dif[2], kds, kd, None, causal),
               softmax_update(dif[3], q_dif[3], kds, kd, None, causal))
        return mla, fox, sb, dif

    def sm0():
        return (jnp.full((tq, 1), NEG_INF, F32), jnp.zeros((tq, 1), F32), jnp.zeros((tq, LANES), F32))

    carry = ((sm0(), sm0()), (sm0(), sm0()),
             ((jnp.zeros((tq, 1), F32), jnp.zeros((tq, LANES), F32)),) * 2,
             (sm0(), sm0(), sm0(), sm0()))
    carry = step(i, carry, True)
    mla, fox, sb, dif = lax.fori_loop(0, i, lambda t, c: step(i - 1 - t, c, False), carry)

    wuv = wuv_ref[...]
    oa = [_dot((acc / l).astype(BF16), wuv) for (_, l, acc) in mla]
    oa_ref[...] = jnp.where(lo, oa[0], oa[1]).astype(BF16)
    ob_ref[...] = jnp.where(lo, fox[0][2] / fox[0][1], fox[1][2] / fox[1][1]).astype(BF16)
    oc_ref[...] = jnp.where(lo, sb[0][1], sb[1][1]).astype(BF16)
    lam = _diff_lambda(lam_ref, lam_init)
    dn = [acc / l for (_, l, acc) in dif]
    od = jnp.where(lo, dn[0] - lam * dn[1], dn[2] - lam * dn[3])
    od2 = od * od
    ss0 = jnp.sum(jnp.where(lo, od2, 0.0), axis=-1, keepdims=True)
    ss1 = jnp.sum(jnp.where(lo, 0.0, od2), axis=-1, keepdims=True)
    inv = jnp.where(lo, lax.rsqrt(ss0 * (1.0 / HD) + EPS), lax.rsqrt(ss1 * (1.0 / HD) + EPS))
    od_ref[...] = (od * inv * dg_ref[...] * (1.0 - lam_init)).astype(BF16)


def _attn_prompt(qmla, qfsd, kmla, ksh, fcc, fcr, wuv, dg, lamv, batch, seq, tq, lam_init):
    m = qmla.shape[0]
    nq = seq // tq
    npair = HEADS // 2
    qrow = lambda off: pl.BlockSpec((tq, LANES), lambda b, p, i: (b * nq + i, off + p))
    out = pl.BlockSpec((tq, LANES), lambda b, p, i: (b * nq + i, p))
    return pl.pallas_call(
        functools.partial(_attn_prompt_kernel, tq=tq, tk=tq, lam_init=lam_init),
        out_shape=tuple(jax.ShapeDtypeStruct((m, GROUP), BF16) for _ in range(4)),
        grid=(batch, npair, nq),
        in_specs=[
            pl.BlockSpec((tq, 2 * LANES), lambda b, p, i: (b * nq + i, p)),
            qrow(0), qrow(npair), qrow(2 * npair),
            pl.BlockSpec((seq, 2 * LANES), lambda b, p, i: (b, p)),
            pl.BlockSpec((seq, KSH_W), lambda b, p, i: (b, 0)),
            pl.BlockSpec((1, tq, LANES), lambda b, p, i: (b, i, 0)),
            pl.BlockSpec((1, nq, HEADS, tq), lambda b, p, i: (b, 0, 0, 0)),
            pl.BlockSpec((MLA_KV_RANK, LANES), lambda b, p, i: (0, p)),
            pl.BlockSpec((1, LANES), lambda b, p, i: (0, 0)),
            pl.BlockSpec((4, LANES), lambda b, p, i: (0, 0)),
        ],
        out_specs=(out, out, out, out),
        compiler_params=_cparams(("parallel", "parallel", "arbitrary")),
        name="attn_prompt",
    )(qmla, qfsd, qfsd, qfsd, kmla, ksh, fcc, fcr, wuv, dg, lamv)


def _out_memq_kernel(x_ref, oa_ref, ob_ref, oc_ref, od_ref, wout_ref, g_ref, wq_ref, qg_ref, x1_ref, qm_ref):
    x1 = x_ref[...]
    for n, o_ref in enumerate((oa_ref, ob_ref, oc_ref, od_ref)):
        x1 = x1 + _dot(o_ref[...], wout_ref[n * GROUP:(n + 1) * GROUP, :])
    x1_ref[...] = x1
    q = _dot(_rms_rows(x1, g_ref[...]).astype(BF16), wq_ref[...])
    qg = qg_ref[...]
    for hh in range(MEM_HEADS):
        sl = slice(hh * MEM_HD, (hh + 1) * MEM_HD)
        qm_ref[:, sl] = (_rms_rows(q[:, sl], qg) * MEM_SCALE).astype(BF16)


def _out_memq(x, o4, wout, g, wq, qg, tm):
    m, d = x.shape
    const = lambda a: pl.BlockSpec(a.shape, lambda i: (0,) * a.ndim)
    row = lambda w: pl.BlockSpec((tm, w), lambda i: (i, 0))
    return pl.pallas_call(
        _out_memq_kernel,
        out_shape=(jax.ShapeDtypeStruct((m, d), F32), jax.ShapeDtypeStruct((m, MEM_HEADS * MEM_HD), BF16)),
        grid=(m // tm,),
        in_specs=[row(d)] + [row(GROUP)] * 4 + [const(wout), const(g), const(wq), const(qg)],
        out_specs=(row(d), row(MEM_HEADS * MEM_HD)),
        compiler_params=_cparams(("parallel",)),
        name="out_proj_memq",
    )(x, *o4, wout, g, wq, qg)


def _memkv_kernel(mem_ref, g_ref, wk_ref, wv_ref, kg_ref, o_ref):
    mn = _rms_rows(mem_ref[...], g_ref[...]).astype(BF16)
    k = _dot(mn, wk_ref[...])
    kg = kg_ref[...]
    w = MEM_HEADS * MEM_HD
    for hh in range(MEM_HEADS):
        sl = slice(hh * MEM_HD, (hh + 1) * MEM_HD)
        o_ref[:, sl] = _rms_rows(k[:, sl], kg)
    o_ref[:, w:2 * w] = _dot(mn, wv_ref[...])


def _memkv(mem, g, wk, wv, kg, tm):
    m, d = mem.shape
    w = MEM_HEADS * MEM_HD
    const = lambda a: pl.BlockSpec(a.shape, lambda i: (0,) * a.ndim)
    return pl.pallas_call(
        _memkv_kernel,
        out_shape=jax.ShapeDtypeStruct((m, 2 * w), F32),
        grid=(m // tm,),
        in_specs=[pl.BlockSpec((tm, d), lambda i: (i, 0)), const(g), const(wk), const(wv), const(kg)],
        out_specs=pl.BlockSpec((tm, 2 * w), lambda i: (i, 0)),
        compiler_params=_cparams(("parallel",)),
        name="memory_kv",
    )(mem, g, wk, wv, kg)


def _mem_prompt_kernel(x1_ref, qm_ref, kv_ref, wo_ref, o_ref):
    w = MEM_HEADS * MEM_HD
    outs = []
    for hh in range(MEM_HEADS):
        k = kv_ref[:, hh * MEM_HD:(hh + 1) * MEM_HD].astype(BF16)
        v = kv_ref[:, w + hh * MEM_HD:w + (hh + 1) * MEM_HD].astype(BF16)
        s = _dot_nt(qm_ref[:, hh * MEM_HD:(hh + 1) * MEM_HD], k)
        e = jnp.exp(s - jnp.max(s, axis=-1, keepdims=True))
        outs.append((_dot(e.astype(BF16), v) / jnp.sum(e, axis=-1, keepdims=True)).astype(BF16))
    o_ref[...] = x1_ref[...] + _dot(jnp.concatenate(outs, axis=1), wo_ref[...])


def _mem_prompt(x1, qm, mkv, wo, batch, seq, n_mem, tm):
    m, d = x1.shape
    w = MEM_HEADS * MEM_HD
    nt = seq // tm
    return pl.pallas_call(
        _mem_prompt_kernel,
        out_shape=jax.ShapeDtypeStruct((m, d), F32),
        grid=(batch, nt),
        in_specs=[
            pl.BlockSpec((tm, d), lambda b, i: (b * nt + i, 0)),
            pl.BlockSpec((tm, w), lambda b, i: (b * nt + i, 0)),
            pl.BlockSpec((n_mem, 2 * w), lambda b, i: (b, 0)),
            pl.BlockSpec(wo.shape, lambda b, i: (0, 0)),
        ],
        out_specs=pl.BlockSpec((tm, d), lambda b, i: (b * nt + i, 0)),
        compiler_params=_cparams(("parallel", "parallel")),
        name="mem_attn_prompt",
    )(x1, qm, mkv, wo)


def _mem_sample_kernel(x1_ref, qm_ref, kv_ref, wo_ref, o_ref, om_ref, *, gsz):
    for g in range(gsz):
        q = qm_ref[g]
        s = jnp.sum(kv_ref[g, :, 0] * q[None], axis=-1, keepdims=True)
        e = jnp.exp(s - jnp.max(s, axis=0, keepdims=True))
        o = jnp.sum(e * kv_ref[g, :, 1], axis=0) / jnp.sum(e, axis=0)
        for hh in range(MEM_HEADS):
            om_ref[hh, g:g + 1, :] = o[hh:hh + 1, :]
    acc = x1_ref[...]
    for hh in range(MEM_HEADS):
        acc = acc + _dot(om_ref[hh].astype(BF16), wo_ref[hh * MEM_HD:(hh + 1) * MEM_HD, :])
    o_ref[...] = acc


def _mem_sample(x1, qm, kv, layer, wo, gsz):
    m, d = x1.shape
    n_mem = kv.shape[2]
    return pl.pallas_call(
        functools.partial(_mem_sample_kernel, gsz=gsz),
        out_shape=jax.ShapeDtypeStruct((m, d), F32),
        grid=(m // gsz,),
        in_specs=[
            pl.BlockSpec((gsz, d), lambda i: (i, 0)),
            pl.BlockSpec((gsz, MEM_HEADS, MEM_HD), lambda i: (i, 0, 0)),
            pl.BlockSpec((None, gsz, n_mem, 2, MEM_HEADS, MEM_HD), lambda i: (layer, i, 0, 0, 0, 0)),
            pl.BlockSpec(wo.shape, lambda i: (0, 0)),
        ],
        out_specs=pl.BlockSpec((gsz, d), lambda i: (i, 0)),
        scratch_shapes=[pltpu.VMEM((MEM_HEADS, gsz, MEM_HD), F32)],
        compiler_params=_cparams(("parallel",)),
        name="mem_attn_sample",
    )(x1, qm, kv, wo)


def _suffix_excl(x, t_after):
    nb = x.shape[1] // LANES
    xs = jnp.concatenate([x[:, j * LANES:(j + 1) * LANES] for j in range(nb)], axis=0)
    hi, lo = _split2(xs)
    ys = _dot(hi, t_after) + _dot(lo, t_after)
    tot = jnp.sum(xs, axis=-1, keepdims=True)
    out = [None] * nb
    carry = jnp.zeros((x.shape[0], 1), F32)
    for j in range(nb - 1, -1, -1):
        out[j] = ys[j * 8:(j + 1) * 8, :] + carry
        carry = carry + tot[j * 8:(j + 1) * 8, :]
    return jnp.concatenate(out, axis=1), carry


def _attn_sample_kernel(pt_ref, q8m_ref, fq8_ref, sq8_ref, dq8_ref, kn8_ref, ksn_ref, lfn_ref,
                        wukt_ref, wuv_ref, dg_ref, lam_ref,
                        c_mla, c_fox, c_sb, c_dif,
                        oa_ref, ob_ref, oc_ref, od_ref,
                        b_mla, b_fox, b_sb, b_dif, sems,
                        m_a, l_a, acc_a, m_f, l_f, acc_f, aft_f, r_s, acc_s, m_d, l_d, acc_d,
                        *, layer, ppc, kb, page, lam_init):
    sm_a, sm_f, sm_d = (m_a, l_a, acc_a), (m_f, l_f, acc_f), (m_d, l_d, acc_d)
    s_id = pl.program_id(0)
    c_id = pl.program_id(1)
    nch = pl.num_programs(1)
    nseq = pl.num_programs(0)
    t = s_id * nch + c_id
    slot = t % 2
    ck = ppc * page
    caches = (c_mla, c_fox, c_sb, c_dif)
    bufs = (b_mla, b_fox, b_sb, b_dif)

    def issue(seq, chunk, sl):
        for pg in range(ppc):
            pid = pt_ref[seq, chunk * ppc + pg]
            keys = pl.ds(pg * page, page)
            for ci in range(4):
                dst = bufs[ci].at[sl, :, keys] if ci < 2 else bufs[ci].at[sl, keys, :]
                pltpu.make_async_copy(caches[ci].at[layer, pid], dst, sems.at[sl, ci]).start()

    @pl.when(t == 0)
    def _():
        issue(0, nch - 1, 0)

    def wait_slot(sl):
        for ci in range(4):
            pltpu.make_async_copy(bufs[ci].at[sl], bufs[ci].at[sl], sems.at[sl, ci]).wait()

    wait_slot(slot)

    lane = lax.broadcasted_iota(jnp.int32, (HEADS, LANES), 1)
    lo = lane < HD
    z8 = jnp.zeros((HEADS, HD), BF16)
    fq8, sq8, dq8 = fq8_ref[0], sq8_ref[0], dq8_ref[0]
    q8m = q8m_ref[0]
    a_sb = jnp.concatenate([sq8, z8], axis=1)
    lane64 = lax.broadcasted_iota(jnp.int32, (HEADS, HD), 1)
    zq = jnp.zeros((HEADS, HD), BF16)
    a_dif = jnp.concatenate([jnp.concatenate([jnp.where(lane64 < ROPE_D, dq8, zq), z8], axis=1),
                             jnp.concatenate([jnp.where(lane64 < ROPE_D, zq, dq8), z8], axis=1)], axis=0)
    sub = lax.broadcasted_iota(jnp.int32, (HEADS, GROUP), 0)
    blk = lax.broadcasted_iota(jnp.int32, (HEADS, GROUP), 1) // HD
    diag = sub == blk
    qn_bd = jnp.where(diag, _tile_lanes(q8m[:, 0:HD], HEADS), jnp.zeros((HEADS, GROUP), BF16))
    a_pe = q8m[:, HD:HD + ROPE_D]
    ident_mask = (lax.broadcasted_iota(jnp.int32, (HEADS, HEADS), 0)
                  == lax.broadcasted_iota(jnp.int32, (HEADS, HEADS), 1))

    @pl.when(c_id == 0)
    def _():
        ksn = ksn_ref[0].astype(F32)
        qf = q8m.astype(F32)
        s_m = jnp.sum(qf * kn8_ref[0].astype(F32), axis=-1, keepdims=True)
        kf = ksn[:, LANES:2 * LANES]
        s_f = jnp.sum(fq8.astype(F32) * kf[:, 0:HD], axis=-1, keepdims=True)
        kd = ksn[:, 5 * LANES:6 * LANES]
        s_d = jnp.sum(a_dif.astype(F32) * kd, axis=-1, keepdims=True)
        for (m_ref, l_ref, acc_ref), s0, v0 in ((sm_a, s_m, ksn[:, 0:LANES]),
                                                (sm_f, s_f, ksn[:, 2 * LANES:2 * LANES + HD]),
                                                (sm_d, s_d, kd)):
            m_ref[...] = s0
            l_ref[...] = jnp.ones_like(l_ref)
            acc_ref[...] = jnp.broadcast_to(v0, acc_ref.shape)
        acc_s[...] = jnp.zeros_like(acc_s)
        r_s[...] = jnp.zeros_like(r_s)
        aft_f[...] = jnp.sum(jnp.where(ident_mask, lfn_ref[0], 0.0), axis=-1, keepdims=True)

    wukt = wukt_ref[...]
    rr = lax.broadcasted_iota(jnp.int32, (LANES, LANES), 0)
    cc = lax.broadcasted_iota(jnp.int32, (LANES, LANES), 1)
    t_after = (rr > cc).astype(BF16)

    def sm_weights(st, s):
        m_ref, l_ref, _ = st
        m = m_ref[...]
        m_new = jnp.maximum(m, jnp.max(s, axis=-1, keepdims=True))
        alpha = jnp.exp(m - m_new)
        e = jnp.exp(s - m_new)
        l_ref[...] = alpha * l_ref[...] + jnp.sum(e, axis=-1, keepdims=True)
        m_ref[...] = m_new
        return alpha, e.astype(BF16)

    fkt = b_fox[slot, 0:HD, :].astype(BF16)
    fvt = b_fox[slot, HD:2 * HD, :].astype(BF16)
    skv = b_sb[slot].astype(BF16)
    dkv = b_dif[slot].astype(BF16)
    f_later, f_total = _suffix_excl(b_fox[slot, 2 * HD:2 * HD + HEADS, :], t_after)
    s_fox = _dot(fq8, fkt) + f_later + aft_f[...]
    z = _dot_nt(a_sb, skv)
    s_dif = _dot_nt(a_dif, dkv)
    lsp = _log_sigmoid(z)
    s_later, s_total = _suffix_excl(lsp - z, t_after)
    ckvt = b_mla[slot, 0:MLA_KV_RANK, :].astype(BF16)
    kpet = b_mla[slot, MLA_KV_RANK:MLA_KV_RANK + ROPE_D, :].astype(BF16)
    qabs = _dot(qn_bd, wukt).astype(BF16)
    n2_parts = []
    for j in range(ck // kb):
        knt = _dot(wukt, ckvt[:, j * kb:(j + 1) * kb])
        k2 = knt * knt
        n2_parts.append(jnp.concatenate(
            [jnp.sum(k2[hh * HD:(hh + 1) * HD, :], axis=0, keepdims=True) for hh in range(HEADS)], axis=0))
    inv = lax.rsqrt(jnp.concatenate(n2_parts, axis=1) * (1.0 / HD) + EPS)
    s_mla = _dot(qabs, ckvt) * inv + _dot(a_pe, kpet)
    last = t + 1 == nseq * nch
    wrap = c_id + 1 == nch
    issue(jnp.where(last, 0, jnp.where(wrap, s_id + 1, s_id)), jnp.where(wrap, nch - 1, nch - 2 - c_id), 1 - slot)
    al_f, e_f = sm_weights(sm_f, s_fox)
    al_d, e_d = sm_weights(sm_d, s_dif)
    w_s = jnp.exp(lsp + s_later + r_s[...]).astype(BF16)
    al_a, e_a = sm_weights(sm_a, s_mla)
    acc_f[...] = al_f * acc_f[...] + _dot_nt(e_f, fvt)
    acc_d[...] = al_d * acc_d[...] + _dot(e_d, dkv)
    acc_s[...] = acc_s[...] + _dot(w_s, skv)
    acc_a[...] = al_a * acc_a[...] + _dot_nt(e_a, ckvt)
    aft_f[...] = aft_f[...] + f_total
    r_s[...] = r_s[...] + s_total

    @pl.when(c_id == nch - 1)
    def _():
        olat = (sm_a[2][...] / sm_a[1][...]).astype(BF16)
        r = _dot(olat, wuv_ref[...])
        oa_ref[0] = jnp.sum(jnp.where(diag, r, 0.0), axis=0, keepdims=True)
        ob_ref[0] = sm_f[2][...] / sm_f[1][...]
        oc_ref[0] = acc_s[...]
        lam = _diff_lambda(lam_ref, lam_init)
        dn = sm_d[2][...] / sm_d[1][...]
        od = dn[0:HEADS] - lam * dn[HEADS:2 * HEADS]
        ss = jnp.sum(jnp.where(lo, 0.0, od * od), axis=-1, keepdims=True)
        od_ref[0] = od * lax.rsqrt(ss * (1.0 / HD) + EPS) * dg_ref[...] * (1.0 - lam_init)

    @pl.when(last)
    def _():
        wait_slot(1 - slot)


def _attn_sample(page_table, q8m, fq8, sq8, dq8, kn8, ksn, lfn, wukt, wuv, dg, lamv, caches, layer, ppc, kb, lam_init):
    ns = q8m.shape[0]
    n_pages = page_table.shape[1]
    page = caches[2].shape[2]
    nch = n_pages // ppc
    ck = ppc * page
    seq3 = lambda a: pl.BlockSpec((1,) + a.shape[1:], lambda s, c, pt: (s, 0, 0))
    const = lambda a: pl.BlockSpec(a.shape, lambda s, c, pt: (0,) * a.ndim)
    anyspec = pl.BlockSpec(memory_space=pl.ANY)
    out8 = lambda w: pl.BlockSpec((1, HEADS, w), lambda s, c, pt: (s, 0, 0))
    grid_spec = pltpu.PrefetchScalarGridSpec(
        num_scalar_prefetch=1,
        grid=(ns, nch),
        in_specs=[seq3(q8m), seq3(fq8), seq3(sq8), seq3(dq8), seq3(kn8), seq3(ksn), seq3(lfn),
                  const(wukt), const(wuv), const(dg), const(lamv), anyspec, anyspec, anyspec, anyspec],
        out_specs=(pl.BlockSpec((1, 1, GROUP), lambda s, c, pt: (s, 0, 0)), out8(HD), out8(LANES), out8(LANES)),
        scratch_shapes=[
            pltpu.VMEM((2, caches[0].shape[2], ck), F32),
            pltpu.VMEM((2, caches[1].shape[2], ck), F32),
            pltpu.VMEM((2, ck, caches[2].shape[3]), F32),
            pltpu.VMEM((2, ck, caches[3].shape[3]), F32),
            pltpu.SemaphoreType.DMA((2, 4)),
            pltpu.VMEM((HEADS, 1), F32), pltpu.VMEM((HEADS, 1), F32), pltpu.VMEM((HEADS, LANES), F32),
            pltpu.VMEM((HEADS, 1), F32), pltpu.VMEM((HEADS, 1), F32), pltpu.VMEM((HEADS, HD), F32),
            pltpu.VMEM((HEADS, 1), F32),
            pltpu.VMEM((HEADS, 1), F32), pltpu.VMEM((HEADS, LANES), F32),
            pltpu.VMEM((2 * HEADS, 1), F32), pltpu.VMEM((2 * HEADS, 1), F32), pltpu.VMEM((2 * HEADS, LANES), F32),
        ],
    )
    return pl.pallas_call(
        functools.partial(_attn_sample_kernel, layer=layer, ppc=ppc, kb=kb, page=page, lam_init=lam_init),
        out_shape=(jax.ShapeDtypeStruct((ns, 1, GROUP), F32), jax.ShapeDtypeStruct((ns, HEADS, HD), F32),
                   jax.ShapeDtypeStruct((ns, HEADS, LANES), F32), jax.ShapeDtypeStruct((ns, HEADS, LANES), F32)),
        grid_spec=grid_spec,
        compiler_params=_cparams(("arbitrary", "arbitrary")),
        name="attn_sample",
    )(page_table, q8m, fq8, sq8, dq8, kn8, ksn, lfn, wukt, wuv, dg, lamv, *caches)


def _group_matrices():
    def pair(groups):
        w = len(groups)
        e = np.zeros((w, LANES), np.float32)
        for i, g in enumerate(groups):
            if g >= 0:
                e[i, g] = 1.0
        return jnp.asarray(e, BF16), jnp.asarray(e.T.copy(), BF16)
    q_groups = [2 * (i // LANES) + (0 if i % LANES < HD else 1) if i % LANES < HD + ROPE_D else -1
                for i in range(HEADS * LANES)]
    k_groups = [i // LANES if i % LANES < HD else -1 for i in range(HEADS * LANES)]
    g64 = [i // HD for i in range(GROUP)]
    g32 = [i // ROPE_D for i in range(GROUP)]
    out = []
    for g in (q_groups, k_groups, g64, g32):
        out.extend(pair(g))
    return tuple(out)


def _rope_tables(pos):
    half = ROPE_D // 2
    inv = ROPE_THETA ** (-jnp.arange(half, dtype=F32) * (2.0 / ROPE_D))
    ang = pos.astype(F32)[:, None] * inv[None, :]
    cos, sin = jnp.cos(ang), jnp.sin(ang)
    n = pos.shape[0]
    cos32 = jnp.concatenate([cos, cos], axis=1)
    sin32 = jnp.concatenate([-sin, sin], axis=1)
    c_m = jnp.concatenate([jnp.ones((n, HD), F32), cos32, jnp.ones((n, LANES - HD - ROPE_D), F32)], axis=1)
    s_m = jnp.concatenate([jnp.zeros((n, HD), F32), sin32, jnp.zeros((n, LANES - HD - ROPE_D), F32)], axis=1)
    c_d = jnp.tile(cos32, (1, LANES // ROPE_D))
    s_d = jnp.tile(sin32, (1, LANES // ROPE_D))
    return jnp.concatenate([c_m, s_m, c_d, s_d], axis=1)


def _pad_cols(a, w):
    return jnp.pad(a, [(0, 0)] * (a.ndim - 1) + [(0, w - a.shape[-1])])


def _gain_table(P):
    L = P["mla_cq_norm"].shape[0]
    ones32 = jnp.ones((L, ROPE_D), F32)
    zeros32 = jnp.zeros((L, ROPE_D), F32)
    rows = [None] * GV_ROWS
    rows[G_CQ] = P["mla_cq_norm"]
    rows[G_CKV] = P["mla_ckv_norm"]
    rows[G_Q1] = jnp.tile(jnp.concatenate([P["mla_qn_gain"], P["mla_qr_gain"], zeros32], axis=1), (1, HEADS))
    rows[G_Q2] = jnp.tile(jnp.concatenate([P["mla_kn_gain"], ones32, zeros32], axis=1), (1, HEADS))
    rows[G_KR] = P["mla_kr_gain"]
    rows[G_FQ] = jnp.tile(P["fox_q_gain"], (1, HEADS))
    rows[G_FK] = P["fox_k_gain"]
    rows[G_FB] = P["fox_f_bias"]
    rows[G_DQ] = jnp.tile(P["diff_q_gain"], (1, 2 * HEADS))
    rows[G_DK] = jnp.tile(P["diff_k_gain"], (1, 2))
    cnt = jnp.tile(jnp.asarray([[1.0 / HD, 1.0 / ROPE_D]], F32), (L, HEADS))
    rows[G_QCNT] = jnp.concatenate([cnt, jnp.ones((L, LANES - 2 * HEADS), F32)], axis=1)
    rows = [jnp.zeros((L, GV_W), F32) if r is None else _pad_cols(r.astype(F32), GV_W) for r in rows]
    return jnp.stack(rows, axis=1)


def _win_padded(w_in):
    parts, start = [], 0
    for s in IN_SIZES:
        parts.append(w_in[..., start:start + s])
        start += s
    cq, ckv, kpe, fq, fk, fv, fz, sq, sk, sv, dq, dk, dv = parts
    cols = [cq, ckv, _pad_cols(kpe, LANES), fq, fk, fv, _pad_cols(fz, LANES), sq, sk, sv, dq, dk, dv]
    return jnp.concatenate(cols, axis=-1).astype(BF16)


def kernel(x_prompt, x_sample, mem_prompt, cache_mla, cache_fox, cache_sb, cache_diff, cache_mem, page_table, ffn1_norm, ffn1_w1, ffn1_w3, ffn1_w2, mix_norm, w_in, fox_f_bias, mla_cq_norm, mla_ckv_norm, mla_w_uq, mla_w_uk, mla_w_uv, mla_qn_gain, mla_qr_gain, mla_kn_gain, mla_kr_gain, fox_q_gain, fox_k_gain, diff_q_gain, diff_k_gain, diff_lam_q1, diff_lam_k1, diff_lam_q2, diff_lam_k2, diff_sub_gain, w_out, mem_q_norm, mem_kv_norm, mem_w_q, mem_w_k, mem_w_v, mem_q_gain, mem_k_gain, mem_w_o, ffn2_norm, ffn2_w1, ffn2_w3, ffn2_w2):
    batch, seq, d = x_prompt.shape
    ns = x_sample.shape[0]
    depth = w_in.shape[0]
    n_mem = mem_prompt.shape[1]
    n_pages = page_table.shape[1]
    page = cache_mla.shape[2]
    past = n_pages * page
    assert x_sample.shape[1] == 1
    mp = batch * seq

    tm_ffn = min(512, seq)
    tf = 512
    tm = min(256, seq)
    tq = min(512, seq)
    ppc = min(32, n_pages)
    kb = min(512, ppc * page)
    gsz = 8

    bf = lambda a: a.astype(BF16)
    f1 = (bf(ffn1_w1), bf(ffn1_w3), bf(ffn1_w2))
    f2 = (bf(ffn2_w1), bf(ffn2_w3), bf(ffn2_w2))
    win_p = _win_padded(w_in)
    wuq_p = bf(_pad_cols(mla_w_uq.reshape(depth, MLA_Q_RANK, HEADS, HD + ROPE_D), LANES)
               .reshape(depth, MLA_Q_RANK, HEADS * LANES))
    wuk_p = bf(_pad_cols(mla_w_uk, LANES).reshape(depth, MLA_KV_RANK, HEADS * LANES))
    wukt = bf(jnp.transpose(mla_w_uk.reshape(depth, MLA_KV_RANK, GROUP), (0, 2, 1)))
    wuv = bf(mla_w_uv.reshape(depth, MLA_KV_RANK, GROUP))
    wout = bf(w_out)
    wq, wk, wv, wo = bf(mem_w_q), bf(mem_w_k), bf(mem_w_v), bf(mem_w_o)
    P = dict(mla_cq_norm=mla_cq_norm, mla_ckv_norm=mla_ckv_norm, mla_qn_gain=mla_qn_gain, mla_qr_gain=mla_qr_gain,
             mla_kn_gain=mla_kn_gain, mla_kr_gain=mla_kr_gain, fox_q_gain=fox_q_gain, fox_k_gain=fox_k_gain,
             fox_f_bias=fox_f_bias, diff_q_gain=diff_q_gain, diff_k_gain=diff_k_gain)
    gv = _gain_table(P)
    emats = _group_matrices()
    lamv = _pad_cols(jnp.stack([diff_lam_q1, diff_lam_k1, diff_lam_q2, diff_lam_k2], axis=1).astype(F32), LANES)
    dg_pair = jnp.tile(diff_sub_gain.astype(F32), (1, 2))[:, None, :]
    dg_hi = jnp.concatenate([jnp.zeros((depth, 1, HD), F32), diff_sub_gain.astype(F32)[:, None, :]], axis=2)
    row1 = lambda a, l: a[l][None, :].astype(F32)

    tab_p = _rope_tables(jnp.tile(jnp.arange(seq, dtype=jnp.int32), batch))
    tab_s = _rope_tables(jnp.full((ns,), past, jnp.int32))
    caches = (jnp.transpose(cache_mla, (0, 1, 3, 2)), jnp.transpose(cache_fox, (0, 1, 3, 2)), cache_sb, cache_diff)

    xp = x_prompt.reshape(mp, d)
    xs = x_sample.reshape(ns, d)
    memf = mem_prompt.reshape(batch * n_mem, d)
    outs_p = [[] for _ in range(5)]
    outs_s = [[] for _ in range(4)]
    for l in range(depth):
        lam_init = 0.8 - 0.6 * math.exp(-0.3 * l)
        mkv = _memkv(memf, row1(mem_kv_norm, l), wk[l], wv[l], row1(mem_k_gain, l), min(256, n_mem))
        xp = _ffn(xp, row1(ffn1_norm, l), *f1, l, tm_ffn, tf)
        qmla, qfsd, kmla, ksh, r_mla, r_fox, r_sb, r_dif = _proj(
            xp, row1(mix_norm, l), win_p[l], wuq_p[l], wuk_p[l], gv[l], tab_p, emats, tm)
        logf = r_fox[:, 2 * HD:2 * HD + HEADS].reshape(batch, seq, HEADS)
        fcc, fcr = _fc(_pad_cols(logf, LANES), jnp.transpose(logf, (0, 2, 1)), tq)
        o4 = _attn_prompt(qmla, qfsd, kmla, ksh, fcc, fcr, wuv[l], dg_pair[l], lamv[l], batch, seq, tq, lam_init)
        x1, qm = _out_memq(xp, o4, wout[l], row1(mem_q_norm, l), wq[l], row1(mem_q_gain, l), tm)
        x2 = _mem_prompt(x1, qm, mkv, wo[l], batch, seq, n_mem, tm)
        xp = _ffn(x2, row1(ffn2_norm, l), *f2, l, tm_ffn, tf)
        for lst, r in zip(outs_p, (r_mla, r_fox, r_sb, r_dif, mkv)):
            lst.append(r)
        xs = _ffn(xs, row1(ffn1_norm, l), *f1, l, ns, tf)
        qmla, qfsd, kmla, ksh, r_mla, r_fox, r_sb, r_dif = _proj(
            xs, row1(mix_norm, l), win_p[l], wuq_p[l], wuk_p[l], gv[l], tab_s, emats, ns)
        q8m = qmla.reshape(ns, HEADS, LANES)
        fq8 = qfsd[:, 0:GROUP].reshape(ns, HEADS, HD)
        sq8 = qfsd[:, GROUP:2 * GROUP].reshape(ns, HEADS, HD)
        dq8 = qfsd[:, 2 * GROUP:3 * GROUP].reshape(ns, HEADS, HD)
        oa, ob, oc, od = _attn_sample(
            page_table, q8m, fq8, sq8, dq8, kmla.reshape(ns, HEADS, LANES), ksh.reshape(ns, 1, KSH_W),
            r_fox[:, 2 * HD:2 * HD + HEADS].reshape(ns, 1, HEADS), wukt[l], wuv[l], dg_hi[l], lamv[l],
            caches, l, ppc, kb, lam_init)
        hi = lambda a: a[:, :, HD:].reshape(ns, GROUP).astype(BF16)
        o4 = (oa.reshape(ns, GROUP).astype(BF16), ob.reshape(ns, GROUP).astype(BF16), hi(oc), hi(od))
        x1, qm = _out_memq(xs, o4, wout[l], row1(mem_q_norm, l), wq[l], row1(mem_q_gain, l), ns)
        x2 = _mem_sample(x1, qm.astype(F32).reshape(ns, MEM_HEADS, MEM_HD), cache_mem, l, wo[l], gsz)
        xs = _ffn(x2, row1(ffn2_norm, l), *f2, l, ns, tf)
        for lst, r in zip(outs_s, (r_mla, r_fox, r_sb, r_dif)):
            lst.append(r)

    stack_p = lambda lst: jnp.stack(lst).reshape(depth, batch, seq, -1)
    stack_s = lambda lst: jnp.stack(lst).reshape(depth, ns, 1, -1)
    return (xp.reshape(batch, seq, d), xs.reshape(ns, 1, d),
            stack_p(outs_p[0]), stack_p(outs_p[1]), stack_p(outs_p[2]), stack_p(outs_p[3]),
            jnp.stack(outs_p[4]).reshape(depth, batch, n_mem, 2, MEM_HEADS, MEM_HD),
            stack_s(outs_s[0]), stack_s(outs_s[1]), stack_s(outs_s[2]), stack_s(outs_s[3]))
```

```python
import functools
import math

import numpy as np
import jax
import jax.numpy as jnp
from jax import lax
from jax.experimental import pallas as pl
from jax.experimental.pallas import tpu as pltpu

F32 = jnp.float32
BF16 = jnp.bfloat16
EPS = 1e-6
ROPE_THETA = 10000.0
NEG_INF = -1e30

LANES = 128
VMEM_LIMIT_BYTES = 56 * 1024 * 1024

HEADS = 8
HD = 64
ROPE_D = 32
MLA_Q_RANK = 384
MLA_KV_RANK = 128
MEM_HEADS = 4
MEM_HD = 128
GROUP = HEADS * HD

MLA_SCALE = (HD + ROPE_D) ** -0.5
FOX_SCALE = HD ** -0.5
SB_SCALE = HD ** -0.5
DIFF_SCALE = ROPE_D ** -0.5
MEM_SCALE = MEM_HD ** -0.5

C_CQ, C_CKV, C_KPE, C_FQ, C_FKV, C_FZ, C_SQ, C_SKV, C_DQ, C_DKV, C_END = (
    0, 384, 512, 640, 1152, 1280, 1408, 1920, 2048, 2560, 2688)
IN_SIZES = (384, 128, 32, 512, 64, 64, 8, 512, 64, 64, 512, 64, 64)

(G_CQ, G_CKV, G_Q1, G_Q2, G_KR, G_FQ, G_FK, G_FB, G_DQ, G_DK, G_QCNT) = range(11)
GV_ROWS = 16
GV_W = 1024


def _cparams(sem):
    return pltpu.CompilerParams(dimension_semantics=sem, vmem_limit_bytes=VMEM_LIMIT_BYTES)


def _dot(a, b):
    return jnp.dot(a, b, preferred_element_type=F32)


def _dot_nt(a, b):
    return lax.dot_general(a, b, (((1,), (1,)), ((), ())), preferred_element_type=F32)


def _rms_rows(x, g):
    return x * lax.rsqrt(jnp.mean(x * x, axis=-1, keepdims=True) + EPS) * g


def _split2(x):
    hi = x.astype(BF16)
    lo = (x - hi.astype(F32)).astype(BF16)
    return hi, lo


def _split3(x):
    h1 = x.astype(BF16)
    r = x - h1.astype(F32)
    h2 = r.astype(BF16)
    h3 = (r - h2.astype(F32)).astype(BF16)
    return h1, h2, h3


def _seg_inv_rms(x, e, et, inv_cnt):
    hi, lo = _split2(x * x)
    ss = _dot(hi, e) + _dot(lo, e)
    inv = lax.rsqrt(ss * inv_cnt + EPS)
    ihi, ilo = _split2(inv)
    return _dot(ihi, et) + _dot(ilo, et)


def _log_sigmoid(z):
    return jnp.minimum(z, 0.0) - jnp.log1p(jnp.exp(-jnp.abs(z)))


def _swap_halves(x):
    w = x.shape[1]
    lane = lax.broadcasted_iota(jnp.int32, x.shape, 1)
    return jnp.where((lane % ROPE_D) < (ROPE_D // 2), pltpu.roll(x, w - ROPE_D // 2, 1),
                     pltpu.roll(x, ROPE_D // 2, 1))


def _tile_lanes(t, n):
    return jnp.concatenate([t] * n, axis=1)


def _ffn_kernel(x_ref, g_ref, w1_ref, w3_ref, w2_ref, o_ref, h_ref, acc_ref):
    f = pl.program_id(1)

    @pl.when(f == 0)
    def _():
        h_ref[...] = _rms_rows(x_ref[...], g_ref[...]).astype(BF16)
        acc_ref[...] = jnp.zeros_like(acc_ref)

    h = h_ref[...]
    a = _dot(h, w1_ref[...])
    b = _dot(h, w3_ref[...])
    act = (a * jax.nn.sigmoid(a)) * b
    acc_ref[...] += _dot(act.astype(BF16), w2_ref[...])

    @pl.when(f == pl.num_programs(1) - 1)
    def _():
        o_ref[...] = x_ref[...] + 0.5 * acc_ref[...]


def _ffn(x, g, w1, w3, w2, layer, tm, tf):
    m, d = x.shape
    dff = w1.shape[2]
    return pl.pallas_call(
        _ffn_kernel,
        out_shape=jax.ShapeDtypeStruct((m, d), F32),
        grid=(m // tm, dff // tf),
        in_specs=[
            pl.BlockSpec((tm, d), lambda i, f: (i, 0)),
            pl.BlockSpec((1, d), lambda i, f: (0, 0)),
            pl.BlockSpec((None, d, tf), lambda i, f: (layer, 0, f)),
            pl.BlockSpec((None, d, tf), lambda i, f: (layer, 0, f)),
            pl.BlockSpec((None, tf, d), lambda i, f: (layer, f, 0)),
        ],
        out_specs=pl.BlockSpec((tm, d), lambda i, f: (i, 0)),
        scratch_shapes=[pltpu.VMEM((tm, d), BF16), pltpu.VMEM((tm, d), F32)],
        compiler_params=_cparams(("parallel", "arbitrary")),
        name="ffn",
    )(x, g, w1, w3, w2)


def _proj_kernel(x_ref, g_ref, win_ref, wuq_ref, wuk_ref, gv_ref, tab_ref,
                 eq_ref, etq_ref, ek_ref, etk_ref, e64_ref, et64_ref, e32_ref, et32_ref,
                 qmla_ref, qfsd_ref, kmla_ref, ksh_ref, mla_row_ref, fox_row_ref, sb_row_ref, diff_row_ref):
    h = _rms_rows(x_ref[...], g_ref[...]).astype(BF16)
    u = _dot(h, win_ref[...])
    tab = tab_ref[...]
    c_m, s_m = tab[:, 0:LANES], tab[:, LANES:2 * LANES]
    c_d, s_d = tab[:, 2 * LANES:3 * LANES], tab[:, 3 * LANES:4 * LANES]
    lane = lax.broadcasted_iota(jnp.int32, (u.shape[0], LANES), 1)
    lo64 = lane < HD

    def rope(x, c, s):
        n = x.shape[1] // LANES
        return x * _tile_lanes(c, n) + _swap_halves(x) * _tile_lanes(s, n)

    cqn = _rms_rows(u[:, C_CQ:C_CKV], gv_ref[G_CQ:G_CQ + 1, 0:MLA_Q_RANK]).astype(BF16)
    q = _dot(cqn, wuq_ref[...])
    inv = _seg_inv_rms(q, eq_ref[...], etq_ref[...], gv_ref[G_QCNT:G_QCNT + 1, 0:LANES])
    qn = q * inv * gv_ref[G_Q1:G_Q1 + 1, :] * gv_ref[G_Q2:G_Q2 + 1, :]
    qmla_ref[...] = (rope(qn, c_m, s_m) * MLA_SCALE).astype(BF16)

    ckvn = _rms_rows(u[:, C_CKV:C_KPE], gv_ref[G_CKV:G_CKV + 1, 0:LANES])
    kp = u[:, C_KPE:C_FQ]
    kpn = kp * lax.rsqrt(jnp.sum(kp * kp, axis=-1, keepdims=True) * (1.0 / ROPE_D) + EPS) \
        * gv_ref[G_KR:G_KR + 1, 0:LANES]
    kpr = rope(kpn, c_d, s_d)
    mla_row_ref[:, 0:LANES] = ckvn
    mla_row_ref[:, LANES:LANES + ROPE_D] = kpr[:, 0:ROPE_D]
    ckvb = ckvn.astype(BF16)
    kn = _dot(ckvb, wuk_ref[...])
    kns = kn * _seg_inv_rms(kn, ek_ref[...], etk_ref[...], 1.0 / HD)
    kmla_ref[...] = (kns + _tile_lanes(pltpu.roll(kpr, HD, 1), HEADS)).astype(BF16)

    fq = u[:, C_FQ:C_FKV]
    fqn = fq * _seg_inv_rms(fq, e64_ref[...], et64_ref[...], 1.0 / HD) * gv_ref[G_FQ:G_FQ + 1, 0:GROUP]
    fkv = u[:, C_FKV:C_FZ]
    ssf = jnp.sum(jnp.where(lo64, fkv * fkv, 0.0), axis=-1, keepdims=True)
    fkvn = jnp.where(lo64, fkv * lax.rsqrt(ssf * (1.0 / HD) + EPS) * gv_ref[G_FK:G_FK + 1, 0:LANES], fkv)
    logf = _log_sigmoid(u[:, C_FZ:C_SQ] + gv_ref[G_FB:G_FB + 1, 0:LANES])
    fox_row_ref[:, 0:LANES] = fkvn
    fox_row_ref[:, LANES:LANES + HEADS] = logf[:, 0:HEADS]

    skv = u[:, C_SKV:C_DQ]
    sb_row_ref[...] = skv

    dq = u[:, C_DQ:C_DKV]
    dqn = dq * _seg_inv_rms(dq, e32_ref[...], et32_ref[...], 1.0 / ROPE_D) * gv_ref[G_DQ:G_DQ + 1, 0:GROUP]
    dkv = u[:, C_DKV:C_END]
    d2 = dkv * dkv
    ss0 = jnp.sum(jnp.where(lane < ROPE_D, d2, 0.0), axis=-1, keepdims=True)
    ss1 = jnp.sum(jnp.where(jnp.logical_and(lane >= ROPE_D, lo64), d2, 0.0), axis=-1, keepdims=True)
    invd = jnp.where(lane < ROPE_D, lax.rsqrt(ss0 * (1.0 / ROPE_D) + EPS), lax.rsqrt(ss1 * (1.0 / ROPE_D) + EPS))
    dkr = rope(dkv * invd * gv_ref[G_DK:G_DK + 1, 0:LANES], c_d, s_d)
    dkvn = jnp.where(lo64, dkr, dkv)
    diff_row_ref[...] = dkvn

    qfsd_ref[:, 0:GROUP] = (fqn * FOX_SCALE).astype(BF16)
    qfsd_ref[:, GROUP:2 * GROUP] = (u[:, C_SQ:C_SKV] * SB_SCALE).astype(BF16)
    qfsd_ref[:, 2 * GROUP:3 * GROUP] = (rope(dqn, c_d, s_d) * DIFF_SCALE).astype(BF16)

    ksh_ref[:, 0:LANES] = ckvb
    for k, v in enumerate((fkvn, skv, dkvn)):
        ksh_ref[:, (1 + 2 * k) * LANES:(2 + 2 * k) * LANES] = v.astype(BF16)
        ksh_ref[:, (2 + 2 * k) * LANES:(3 + 2 * k) * LANES] = pltpu.roll(v, HD, 1).astype(BF16)


KSH_W = 7 * LANES
DECODE_SLOTS = 3


def _proj(x, g, win, wuq, wuk, gv, tab, emats, tm):
    m, d = x.shape
    const = lambda a: pl.BlockSpec(a.shape, lambda i: (0,) * a.ndim)
    row = lambda w: pl.BlockSpec((tm, w), lambda i: (i, 0))
    out_w = (8 * LANES, 3 * GROUP, 8 * LANES, KSH_W, 160, 136, 128, 128)
    out_dt = (BF16, BF16, BF16, BF16, F32, F32, F32, F32)
    return pl.pallas_call(
        _proj_kernel,
        out_shape=tuple(jax.ShapeDtypeStruct((m, w), t) for w, t in zip(out_w, out_dt)),
        grid=(m // tm,),
        in_specs=[row(d), const(g), const(win), const(wuq), const(wuk), const(gv), row(4 * LANES)]
                 + [const(e) for e in emats],
        out_specs=tuple(row(w) for w in out_w),
        compiler_params=_cparams(("parallel",)),
        name="mixer_proj",
    )(x, g, win, wuq, wuk, gv, tab, *emats)


def _fc_kernel(lc_ref, lr_ref, fcc_ref, fcr_ref, *, tb):
    s = lc_ref.shape[1]
    r = lax.broadcasted_iota(jnp.int32, (tb, tb), 0)
    c = lax.broadcasted_iota(jnp.int32, (tb, tb), 1)
    tril = (c <= r).astype(BF16)
    triu = (r <= c).astype(BF16)
    carry_c = jnp.zeros((1, LANES), F32)
    carry_r = jnp.zeros((HEADS, 1), F32)
    for j in range(s // tb):
        xc = lc_ref[0, j * tb:(j + 1) * tb, :]
        cs = sum(_dot(tril, p) for p in _split3(xc)) + carry_c
        fcc_ref[0, j * tb:(j + 1) * tb, :] = cs
        carry_c = cs[tb - 1:tb, :]
        xr = lr_ref[0, :, j * tb:(j + 1) * tb]
        csr = sum(_dot(p, triu) for p in _split3(xr)) + carry_r
        fcr_ref[0, j] = csr
        carry_r = csr[:, tb - 1:tb]


def _fc(logf_col, logf_row, tb):
    b, s, _ = logf_col.shape
    return pl.pallas_call(
        functools.partial(_fc_kernel, tb=tb),
        out_shape=(jax.ShapeDtypeStruct((b, s, LANES), F32), jax.ShapeDtypeStruct((b, s // tb, HEADS, tb), F32)),
        grid=(b,),
        in_specs=[pl.BlockSpec((1, s, LANES), lambda i: (i, 0, 0)), pl.BlockSpec((1, HEADS, s), lambda i: (i, 0, 0))],
        out_specs=(pl.BlockSpec((1, s, LANES), lambda i: (i, 0, 0)),
                   pl.BlockSpec((1, s // tb, HEADS, tb), lambda i: (i, 0, 0, 0))),
        compiler_params=_cparams(("parallel",)),
        name="forget_prefix",
    )(logf_col, logf_row)


def _diff_lambda(lam_ref, lam_init):
    a = jnp.sum(lam_ref[0:1, :] * lam_ref[1:2, :], axis=-1, keepdims=True)
    b = jnp.sum(lam_ref[2:3, :] * lam_ref[3:4, :], axis=-1, keepdims=True)
    return jnp.exp(a) - jnp.exp(b) + lam_init


def _attn_prompt_kernel(qm_ref, fq_ref, sq_ref, dq_ref, km_ref, ks_ref, fcc_ref, fcr_ref, wuv_ref, dg_ref, lam_ref,
                        oa_ref, ob_ref, oc_ref, od_ref, *, tq, tk, lam_init):
    p = pl.program_id(1)
    i = pl.program_id(2)
    lane = lax.broadcasted_iota(jnp.int32, (tq, LANES), 1)
    lo = lane < HD
    fqp, sqp, dqp = (r[...].astype(F32) for r in (fq_ref, sq_ref, dq_ref))
    q_mla = (qm_ref[:, 0:LANES], qm_ref[:, LANES:2 * LANES])
    q_fox = (jnp.where(lo, fqp, 0.0).astype(BF16), jnp.where(lo, 0.0, fqp).astype(BF16))
    q_sb = (jnp.where(lo, sqp, 0.0).astype(BF16), jnp.where(lo, 0.0, sqp).astype(BF16))
    q_dif = tuple(jnp.where(jnp.logical_and(lane >= ROPE_D * c, lane < ROPE_D * (c + 1)), dqp, 0.0).astype(BF16)
                  for c in range(4))

    fcc = fcc_ref[0]
    fcq = tuple(jnp.sum(jnp.where(lane == 2 * p + r, fcc, 0.0), axis=-1, keepdims=True) for r in range(2))

    rr = lax.broadcasted_iota(jnp.int32, (tk, tk), 0)
    cc = lax.broadcasted_iota(jnp.int32, (tk, tk), 1)
    t_after = (rr > cc).astype(BF16)

    def softmax_update(st, q, k, v, bias, mask):
        m, l, acc = st
        s = _dot_nt(q, k)
        if bias is not None:
            s = s + bias
        if mask is not None:
            s = jnp.where(mask, s, NEG_INF)
        m_new = jnp.maximum(m, jnp.max(s, axis=-1, keepdims=True))
        alpha = jnp.exp(m - m_new)
        e = jnp.exp(s - m_new)
        l = alpha * l + jnp.sum(e, axis=-1, keepdims=True)
        acc = alpha * acc + _dot(e.astype(BF16), v)
        return m_new, l, acc

    def sb_update(st, q, k, v, strict):
        r_sum, acc = st
        z = _dot_nt(q, k)
        lsp = _log_sigmoid(z)
        lsn = lsp - z
        if strict is not None:
            lsn = jnp.where(strict, lsn, 0.0)
        hi, lo2 = _split2(lsn)
        cs = _dot(hi, t_after) + _dot(lo2, t_after)
        w = jnp.exp(lsp + cs + r_sum)
        if strict is not None:
            w = jnp.where(strict, w, 0.0)
        acc = acc + _dot(w.astype(BF16), v)
        return r_sum + jnp.sum(lsn, axis=-1, keepdims=True), acc

    def step(j, carry, masked):
        mla, fox, sb, dif = carry
        ks0 = pl.multiple_of(j * tk, tk)
        kmb = km_ref[pl.ds(ks0, tk), :]
        ksb = ks_ref[pl.ds(ks0, tk), :]
        seg = lambda n: ksb[:, n * LANES:(n + 1) * LANES]
        ckv, kf, kfs, ksk, kss, kd, kds = (seg(n) for n in range(7))
        causal = strict = None
        if masked:
            row = lax.broadcasted_iota(jnp.int32, (tq, tk), 0)
            col = lax.broadcasted_iota(jnp.int32, (tq, tk), 1)
            causal, strict = col <= row, col < row
        mla = tuple(softmax_update(mla[r], q_mla[r], kmb[:, r * LANES:(r + 1) * LANES], ckv, None, causal)
                    for r in range(2))
        fck = tuple(fcr_ref[0, j, pl.ds(2 * p + r, 1), :] for r in range(2))
        fox = (softmax_update(fox[0], q_fox[0], kf, kfs, fcq[0] - fck[0], causal),
               softmax_update(fox[1], q_fox[1], kfs, kf, fcq[1] - fck[1], causal))
        sb = (sb_update(sb[0], q_sb[0], ksk, kss, strict), sb_update(sb[1], q_sb[1], kss, ksk, strict))
        dif = (softmax_update(dif[0], q_dif[0], kd, kds, None, causal),
               softmax_update(dif[1], q_dif[1], kd, kds, None, causal),
               softmax_update(dif[2], q_dif[2], kds, kd, None, causal),
               softmax_update(dif[3], q_dif[3], kds, kd, None, causal))
        return mla, fox, sb, dif

    def sm0():
        return (jnp.full((tq, 1), NEG_INF, F32), jnp.zeros((tq, 1), F32), jnp.zeros((tq, LANES), F32))

    carry = ((sm0(), sm0()), (sm0(), sm0()),
             ((jnp.zeros((tq, 1), F32), jnp.zeros((tq, LANES), F32)),) * 2,
             (sm0(), sm0(), sm0(), sm0()))
    carry = step(i, carry, True)
    mla, fox, sb, dif = lax.fori_loop(0, i, lambda t, c: step(i - 1 - t, c, False), carry)

    wuv = wuv_ref[...]
    oa = [_dot((acc / l).astype(BF16), wuv) for (_, l, acc) in mla]
    oa_ref[...] = jnp.where(lo, oa[0], oa[1]).astype(BF16)
    ob_ref[...] = jnp.where(lo, fox[0][2] / fox[0][1], fox[1][2] / fox[1][1]).astype(BF16)
    oc_ref[...] = jnp.where(lo, sb[0][1], sb[1][1]).astype(BF16)
    lam = _diff_lambda(lam_ref, lam_init)
    dn = [acc / l for (_, l, acc) in dif]
    od = jnp.where(lo, dn[0] - lam * dn[1], dn[2] - lam * dn[3])
    od2 = od * od
    ss0 = jnp.sum(jnp.where(lo, od2, 0.0), axis=-1, keepdims=True)
    ss1 = jnp.sum(jnp.where(lo, 0.0, od2), axis=-1, keepdims=True)
    inv = jnp.where(lo, lax.rsqrt(ss0 * (1.0 / HD) + EPS), lax.rsqrt(ss1 * (1.0 / HD) + EPS))
    od_ref[...] = (od * inv * dg_ref[...] * (1.0 - lam_init)).astype(BF16)


def _attn_prompt(qmla, qfsd, kmla, ksh, fcc, fcr, wuv, dg, lamv, batch, seq, tq, lam_init):
    m = qmla.shape[0]
    nq = seq // tq
    npair = HEADS // 2
    qrow = lambda off: pl.BlockSpec((tq, LANES), lambda b, p, i: (b * nq + i, off + p))
    out = pl.BlockSpec((tq, LANES), lambda b, p, i: (b * nq + i, p))
    return pl.pallas_call(
        functools.partial(_attn_prompt_kernel, tq=tq, tk=tq, lam_init=lam_init),
        out_shape=tuple(jax.ShapeDtypeStruct((m, GROUP), BF16) for _ in range(4)),
        grid=(batch, npair, nq),
        in_specs=[
            pl.BlockSpec((tq, 2 * LANES), lambda b, p, i: (b * nq + i, p)),
            qrow(0), qrow(npair), qrow(2 * npair),
            pl.BlockSpec((seq, 2 * LANES), lambda b, p, i: (b, p)),
            pl.BlockSpec((seq, KSH_W), lambda b, p, i: (b, 0)),
            pl.BlockSpec((1, tq, LANES), lambda b, p, i: (b, i, 0)),
            pl.BlockSpec((1, nq, HEADS, tq), lambda b, p, i: (b, 0, 0, 0)),
            pl.BlockSpec((MLA_KV_RANK, LANES), lambda b, p, i: (0, p)),
            pl.BlockSpec((1, LANES), lambda b, p, i: (0, 0)),
            pl.BlockSpec((4, LANES), lambda b, p, i: (0, 0)),
        ],
        out_specs=(out, out, out, out),
        compiler_params=_cparams(("parallel", "parallel", "arbitrary")),
        name="attn_prompt",
    )(qmla, qfsd, qfsd, qfsd, kmla, ksh, fcc, fcr, wuv, dg, lamv)


def _out_memq_kernel(x_ref, oa_ref, ob_ref, oc_ref, od_ref, wout_ref, g_ref, wq_ref, qg_ref, x1_ref, qm_ref):
    x1 = x_ref[...]
    for n, o_ref in enumerate((oa_ref, ob_ref, oc_ref, od_ref)):
        x1 = x1 + _dot(o_ref[...], wout_ref[n * GROUP:(n + 1) * GROUP, :])
    x1_ref[...] = x1
    q = _dot(_rms_rows(x1, g_ref[...]).astype(BF16), wq_ref[...])
    qg = qg_ref[...]
    for hh in range(MEM_HEADS):
        sl = slice(hh * MEM_HD, (hh + 1) * MEM_HD)
        qm_ref[:, sl] = (_rms_rows(q[:, sl], qg) * MEM_SCALE).astype(BF16)


def _out_memq(x, o4, wout, g, wq, qg, tm):
    m, d = x.shape
    const = lambda a: pl.BlockSpec(a.shape, lambda i: (0,) * a.ndim)
    row = lambda w: pl.BlockSpec((tm, w), lambda i: (i, 0))
    return pl.pallas_call(
        _out_memq_kernel,
        out_shape=(jax.ShapeDtypeStruct((m, d), F32), jax.ShapeDtypeStruct((m, MEM_HEADS * MEM_HD), BF16)),
        grid=(m // tm,),
        in_specs=[row(d)] + [row(GROUP)] * 4 + [const(wout), const(g), const(wq), const(qg)],
        out_specs=(row(d), row(MEM_HEADS * MEM_HD)),
        compiler_params=_cparams(("parallel",)),
        name="out_proj_memq",
    )(x, *o4, wout, g, wq, qg)


def _memkv_kernel(mem_ref, g_ref, wk_ref, wv_ref, kg_ref, o_ref):
    mn = _rms_rows(mem_ref[...], g_ref[...]).astype(BF16)
    k = _dot(mn, wk_ref[...])
    kg = kg_ref[...]
    w = MEM_HEADS * MEM_HD
    for hh in range(MEM_HEADS):
        sl = slice(hh * MEM_HD, (hh + 1) * MEM_HD)
        o_ref[:, sl] = _rms_rows(k[:, sl], kg)
    o_ref[:, w:2 * w] = _dot(mn, wv_ref[...])


def _memkv(mem, g, wk, wv, kg, tm):
    m, d = mem.shape
    w = MEM_HEADS * MEM_HD
    const = lambda a: pl.BlockSpec(a.shape, lambda i: (0,) * a.ndim)
    return pl.pallas_call(
        _memkv_kernel,
        out_shape=jax.ShapeDtypeStruct((m, 2 * w), F32),
        grid=(m // tm,),
        in_specs=[pl.BlockSpec((tm, d), lambda i: (i, 0)), const(g), const(wk), const(wv), const(kg)],
        out_specs=pl.BlockSpec((tm, 2 * w), lambda i: (i, 0)),
        compiler_params=_cparams(("parallel",)),
        name="memory_kv",
    )(mem, g, wk, wv, kg)


def _mem_prompt_kernel(x1_ref, qm_ref, kv_ref, wo_ref, o_ref):
    w = MEM_HEADS * MEM_HD
    outs = []
    for hh in range(MEM_HEADS):
        k = kv_ref[:, hh * MEM_HD:(hh + 1) * MEM_HD].astype(BF16)
        v = kv_ref[:, w + hh * MEM_HD:w + (hh + 1) * MEM_HD].astype(BF16)
        s = _dot_nt(qm_ref[:, hh * MEM_HD:(hh + 1) * MEM_HD], k)
        e = jnp.exp(s - jnp.max(s, axis=-1, keepdims=True))
        outs.append((_dot(e.astype(BF16), v) / jnp.sum(e, axis=-1, keepdims=True)).astype(BF16))
    o_ref[...] = x1_ref[...] + _dot(jnp.concatenate(outs, axis=1), wo_ref[...])


def _mem_prompt(x1, qm, mkv, wo, batch, seq, n_mem, tm):
    m, d = x1.shape
    w = MEM_HEADS * MEM_HD
    nt = seq // tm
    return pl.pallas_call(
        _mem_prompt_kernel,
        out_shape=jax.ShapeDtypeStruct((m, d), F32),
        grid=(batch, nt),
        in_specs=[
            pl.BlockSpec((tm, d), lambda b, i: (b * nt + i, 0)),
            pl.BlockSpec((tm, w), lambda b, i: (b * nt + i, 0)),
            pl.BlockSpec((n_mem, 2 * w), lambda b, i: (b, 0)),
            pl.BlockSpec(wo.shape, lambda b, i: (0, 0)),
        ],
        out_specs=pl.BlockSpec((tm, d), lambda b, i: (b * nt + i, 0)),
        compiler_params=_cparams(("parallel", "parallel")),
        name="mem_attn_prompt",
    )(x1, qm, mkv, wo)


def _mem_sample_kernel(x1_ref, qm_ref, kv_ref, wo_ref, o_ref, om_ref, *, gsz):
    for g in range(gsz):
        q = qm_ref[g]
        s = jnp.sum(kv_ref[g, :, 0] * q[None], axis=-1, keepdims=True)
        e = jnp.exp(s - jnp.max(s, axis=0, keepdims=True))
        o = jnp.sum(e * kv_ref[g, :, 1], axis=0) / jnp.sum(e, axis=0)
        for hh in range(MEM_HEADS):
            om_ref[hh, g:g + 1, :] = o[hh:hh + 1, :]
    acc = x1_ref[...]
    for hh in range(MEM_HEADS):
        acc = acc + _dot(om_ref[hh].astype(BF16), wo_ref[hh * MEM_HD:(hh + 1) * MEM_HD, :])
    o_ref[...] = acc


def _mem_sample(x1, qm, kv, layer, wo, gsz):
    m, d = x1.shape
    n_mem = kv.shape[2]
    return pl.pallas_call(
        functools.partial(_mem_sample_kernel, gsz=gsz),
        out_shape=jax.ShapeDtypeStruct((m, d), F32),
        grid=(m // gsz,),
        in_specs=[
            pl.BlockSpec((gsz, d), lambda i: (i, 0)),
            pl.BlockSpec((gsz, MEM_HEADS, MEM_HD), lambda i: (i, 0, 0)),
            pl.BlockSpec((None, gsz, n_mem, 2, MEM_HEADS, MEM_HD), lambda i: (layer, i, 0, 0, 0, 0)),
            pl.BlockSpec(wo.shape, lambda i: (0, 0)),
        ],
        out_specs=pl.BlockSpec((gsz, d), lambda i: (i, 0)),
        scratch_shapes=[pltpu.VMEM((MEM_HEADS, gsz, MEM_HD), F32)],
        compiler_params=_cparams(("parallel",)),
        name="mem_attn_sample",
    )(x1, qm, kv, wo)


def _suffix_excl(x, t_after):
    nb = x.shape[1] // LANES
    xs = jnp.concatenate([x[:, j * LANES:(j + 1) * LANES] for j in range(nb)], axis=0)
    hi, lo = _split2(xs)
    ys = _dot(hi, t_after) + _dot(lo, t_after)
    tot = jnp.sum(xs, axis=-1, keepdims=True)
    out = [None] * nb
    carry = jnp.zeros((x.shape[0], 1), F32)
    for j in range(nb - 1, -1, -1):
        out[j] = ys[j * 8:(j + 1) * 8, :] + carry
        carry = carry + tot[j * 8:(j + 1) * 8, :]
    return jnp.concatenate(out, axis=1), carry


def _attn_sample_kernel(pt_ref, q8m_ref, fq8_ref, sq8_ref, dq8_ref, kn8_ref, ksn_ref, lfn_ref,
                        wukt_ref, wuv_ref, dg_ref, lam_ref,
                        c_mla, c_fox, c_sb, c_dif,
                        oa_ref, ob_ref, oc_ref, od_ref,
                        b_mla, b_fox, b_sb, b_dif, sems,
                        m_a, l_a, acc_a, m_f, l_f, acc_f, aft_f, r_s, acc_s, m_d, l_d, acc_d,
                        *, layer, nseq, nch, ppc, kb, page, lam_init):
    sm_a, sm_f, sm_d = (m_a, l_a, acc_a), (m_f, l_f, acc_f), (m_d, l_d, acc_d)
    s_id = pl.program_id(0)
    c_id = pl.program_id(1)
    total = nseq * nch
    t = s_id * nch + c_id
    slot = lax.rem(t, DECODE_SLOTS)
    ck = ppc * page
    caches = (c_mla, c_fox, c_sb, c_dif)
    bufs = (b_mla, b_fox, b_sb, b_dif)

    def issue(seq, chunk, sl):
        for pg in range(ppc):
            pid = pt_ref[seq, chunk * ppc + pg]
            keys = pl.ds(pg * page, page)
            for ci in range(4):
                dst = bufs[ci].at[sl, :, keys] if ci < 2 else bufs[ci].at[sl, keys, :]
                pltpu.make_async_copy(caches[ci].at[layer, pid], dst, sems.at[sl, ci]).start()

    def step_pages(tt):
        tt = jnp.where(tt < total, tt, 0)
        seq = tt // nch
        return seq, nch - 1 - (tt - seq * nch)

    @pl.when(t == 0)
    def _():
        for a in range(DECODE_SLOTS - 1):
            issue(a // nch, nch - 1 - a % nch, a)

    def wait_slot(sl):
        for ci in range(4):
            pltpu.make_async_copy(bufs[ci].at[sl], bufs[ci].at[sl], sems.at[sl, ci]).wait()

    wait_slot(slot)

    lane = lax.broadcasted_iota(jnp.int32, (HEADS, LANES), 1)
    lo = lane < HD
    z8 = jnp.zeros((HEADS, HD), BF16)
    fq8, sq8, dq8 = fq8_ref[0], sq8_ref[0], dq8_ref[0]
    q8m = q8m_ref[0]
    a_sb = jnp.concatenate([sq8, z8], axis=1)
    lane64 = lax.broadcasted_iota(jnp.int32, (HEADS, HD), 1)
    zq = jnp.zeros((HEADS, HD), BF16)
    a_dif = jnp.concatenate([jnp.concatenate([jnp.where(lane64 < ROPE_D, dq8, zq), z8], axis=1),
                             jnp.concatenate([jnp.where(lane64 < ROPE_D, zq, dq8), z8], axis=1)], axis=0)
    sub = lax.broadcasted_iota(jnp.int32, (HEADS, GROUP), 0)
    blk = lax.broadcasted_iota(jnp.int32, (HEADS, GROUP), 1) // HD
    diag = sub == blk
    qn_bd = jnp.where(diag, _tile_lanes(q8m[:, 0:HD], HEADS), jnp.zeros((HEADS, GROUP), BF16))
    a_pe = q8m[:, HD:HD + ROPE_D]
    ident_mask = (lax.broadcasted_iota(jnp.int32, (HEADS, HEADS), 0)
                  == lax.broadcasted_iota(jnp.int32, (HEADS, HEADS), 1))

    @pl.when(c_id == 0)
    def _():
        ksn = ksn_ref[0].astype(F32)
        qf = q8m.astype(F32)
        s_m = jnp.sum(qf * kn8_ref[0].astype(F32), axis=-1, keepdims=True)
        kf = ksn[:, LANES:2 * LANES]
        s_f = jnp.sum(fq8.astype(F32) * kf[:, 0:HD], axis=-1, keepdims=True)
        kd = ksn[:, 5 * LANES:6 * LANES]
        s_d = jnp.sum(a_dif.astype(F32) * kd, axis=-1, keepdims=True)
        for (m_ref, l_ref, acc_ref), s0, v0 in ((sm_a, s_m, ksn[:, 0:LANES]),
                                                (sm_f, s_f, ksn[:, 2 * LANES:2 * LANES + HD]),
                                                (sm_d, s_d, kd)):
            m_ref[...] = s0
            l_ref[...] = jnp.ones_like(l_ref)
            acc_ref[...] = jnp.broadcast_to(v0, acc_ref.shape)
        acc_s[...] = jnp.zeros_like(acc_s)
        r_s[...] = jnp.zeros_like(r_s)
        aft_f[...] = jnp.sum(jnp.where(ident_mask, lfn_ref[0], 0.0), axis=-1, keepdims=True)

    wukt = wukt_ref[...]
    rr = lax.broadcasted_iota(jnp.int32, (LANES, LANES), 0)
    cc = lax.broadcasted_iota(jnp.int32, (LANES, LANES), 1)
    t_after = (rr > cc).astype(BF16)

    def sm_weights(st, s):
        m_ref, l_ref, _ = st
        m = m_ref[...]
        m_new = jnp.maximum(m, jnp.max(s, axis=-1, keepdims=True))
        alpha = jnp.exp(m - m_new)
        e = jnp.exp(s - m_new)
        l_ref[...] = alpha * l_ref[...] + jnp.sum(e, axis=-1, keepdims=True)
        m_ref[...] = m_new
        return alpha, e.astype(BF16)

    fkt = b_fox[slot, 0:HD, :].astype(BF16)
    fvt = b_fox[slot, HD:2 * HD, :].astype(BF16)
    skv = b_sb[slot].astype(BF16)
    dkv = b_dif[slot].astype(BF16)
    f_later, f_total = _suffix_excl(b_fox[slot, 2 * HD:2 * HD + HEADS, :], t_after)
    s_fox = _dot(fq8, fkt) + f_later + aft_f[...]
    z = _dot_nt(a_sb, skv)
    s_dif = _dot_nt(a_dif, dkv)
    lsp = _log_sigmoid(z)
    s_later, s_total = _suffix_excl(lsp - z, t_after)
    ckvt = b_mla[slot, 0:MLA_KV_RANK, :].astype(BF16)
    kpet = b_mla[slot, MLA_KV_RANK:MLA_KV_RANK + ROPE_D, :].astype(BF16)
    qabs = _dot(qn_bd, wukt).astype(BF16)
    n2_parts = []
    for j in range(ck // kb):
        knt = _dot(wukt, ckvt[:, j * kb:(j + 1) * kb])
        k2 = knt * knt
        n2_parts.append(jnp.concatenate(
            [jnp.sum(k2[hh * HD:(hh + 1) * HD, :], axis=0, keepdims=True) for hh in range(HEADS)], axis=0))
    inv = lax.rsqrt(jnp.concatenate(n2_parts, axis=1) * (1.0 / HD) + EPS)
    s_mla = _dot(qabs, ckvt) * inv + _dot(a_pe, kpet)
    ahead = t + (DECODE_SLOTS - 1)
    issue(*step_pages(ahead), lax.rem(ahead, DECODE_SLOTS))
    al_f, e_f = sm_weights(sm_f, s_fox)
    al_d, e_d = sm_weights(sm_d, s_dif)
    w_s = jnp.exp(lsp + s_later + r_s[...]).astype(BF16)
    al_a, e_a = sm_weights(sm_a, s_mla)
    acc_f[...] = al_f * acc_f[...] + _dot_nt(e_f, fvt)
    acc_d[...] = al_d * acc_d[...] + _dot(e_d, dkv)
    acc_s[...] = acc_s[...] + _dot(w_s, skv)
    acc_a[...] = al_a * acc_a[...] + _dot_nt(e_a, ckvt)
    aft_f[...] = aft_f[...] + f_total
    r_s[...] = r_s[...] + s_total

    @pl.when(c_id == nch - 1)
    def _():
        olat = (sm_a[2][...] / sm_a[1][...]).astype(BF16)
        r = _dot(olat, wuv_ref[...])
        oa_ref[0] = jnp.sum(jnp.where(diag, r, 0.0), axis=0, keepdims=True)
        ob_ref[0] = sm_f[2][...] / sm_f[1][...]
        oc_ref[0] = acc_s[...]
        lam = _diff_lambda(lam_ref, lam_init)
        dn = sm_d[2][...] / sm_d[1][...]
        od = dn[0:HEADS] - lam * dn[HEADS:2 * HEADS]
        ss = jnp.sum(jnp.where(lo, 0.0, od * od), axis=-1, keepdims=True)
        od_ref[0] = od * lax.rsqrt(ss * (1.0 / HD) + EPS) * dg_ref[...] * (1.0 - lam_init)

    @pl.when(t == total - 1)
    def _():
        for a in range(1, DECODE_SLOTS):
            wait_slot(lax.rem(t + a, DECODE_SLOTS))


def _attn_sample(page_table, q8m, fq8, sq8, dq8, kn8, ksn, lfn, wukt, wuv, dg, lamv, caches, layer, ppc, kb, lam_init):
    ns = q8m.shape[0]
    n_pages = page_table.shape[1]
    page = caches[2].shape[2]
    nch = n_pages // ppc
    ck = ppc * page
    seq3 = lambda a: pl.BlockSpec((1,) + a.shape[1:], lambda s, c, pt: (s, 0, 0))
    const = lambda a: pl.BlockSpec(a.shape, lambda s, c, pt: (0,) * a.ndim)
    anyspec = pl.BlockSpec(memory_space=pl.ANY)
    out8 = lambda w: pl.BlockSpec((1, HEADS, w), lambda s, c, pt: (s, 0, 0))
    grid_spec = pltpu.PrefetchScalarGridSpec(
        num_scalar_prefetch=1,
        grid=(ns, nch),
        in_specs=[seq3(q8m), seq3(fq8), seq3(sq8), seq3(dq8), seq3(kn8), seq3(ksn), seq3(lfn),
                  const(wukt), const(wuv), const(dg), const(lamv), anyspec, anyspec, anyspec, anyspec],
        out_specs=(pl.BlockSpec((1, 1, GROUP), lambda s, c, pt: (s, 0, 0)), out8(HD), out8(LANES), out8(LANES)),
        scratch_shapes=[
            pltpu.VMEM((DECODE_SLOTS, caches[0].shape[2], ck), F32),
            pltpu.VMEM((DECODE_SLOTS, caches[1].shape[2], ck), F32),
            pltpu.VMEM((DECODE_SLOTS, ck, caches[2].shape[3]), F32),
            pltpu.VMEM((DECODE_SLOTS, ck, caches[3].shape[3]), F32),
            pltpu.SemaphoreType.DMA((DECODE_SLOTS, 4)),
            pltpu.VMEM((HEADS, 1), F32), pltpu.VMEM((HEADS, 1), F32), pltpu.VMEM((HEADS, LANES), F32),
            pltpu.VMEM((HEADS, 1), F32), pltpu.VMEM((HEADS, 1), F32), pltpu.VMEM((HEADS, HD), F32),
            pltpu.VMEM((HEADS, 1), F32),
            pltpu.VMEM((HEADS, 1), F32), pltpu.VMEM((HEADS, LANES), F32),
            pltpu.VMEM((2 * HEADS, 1), F32), pltpu.VMEM((2 * HEADS, 1), F32), pltpu.VMEM((2 * HEADS, LANES), F32),
        ],
    )
    return pl.pallas_call(
        functools.partial(_attn_sample_kernel, layer=layer, nseq=ns, nch=nch, ppc=ppc, kb=kb, page=page,
                          lam_init=lam_init),
        out_shape=(jax.ShapeDtypeStruct((ns, 1, GROUP), F32), jax.ShapeDtypeStruct((ns, HEADS, HD), F32),
                   jax.ShapeDtypeStruct((ns, HEADS, LANES), F32), jax.ShapeDtypeStruct((ns, HEADS, LANES), F32)),
        grid_spec=grid_spec,
        compiler_params=_cparams(("arbitrary", "arbitrary")),
        name="attn_sample",
    )(page_table, q8m, fq8, sq8, dq8, kn8, ksn, lfn, wukt, wuv, dg, lamv, *caches)


def _group_matrices():
    def pair(groups):
        w = len(groups)
        e = np.zeros((w, LANES), np.float32)
        for i, g in enumerate(groups):
            if g >= 0:
                e[i, g] = 1.0
        return jnp.asarray(e, BF16), jnp.asarray(e.T.copy(), BF16)
    q_groups = [2 * (i // LANES) + (0 if i % LANES < HD else 1) if i % LANES < HD + ROPE_D else -1
                for i in range(HEADS * LANES)]
    k_groups = [i // LANES if i % LANES < HD else -1 for i in range(HEADS * LANES)]
    g64 = [i // HD for i in range(GROUP)]
    g32 = [i // ROPE_D for i in range(GROUP)]
    out = []
    for g in (q_groups, k_groups, g64, g32):
        out.extend(pair(g))
    return tuple(out)


def _rope_tables(pos):
    half = ROPE_D // 2
    inv = ROPE_THETA ** (-jnp.arange(half, dtype=F32) * (2.0 / ROPE_D))
    ang = pos.astype(F32)[:, None] * inv[None, :]
    cos, sin = jnp.cos(ang), jnp.sin(ang)
    n = pos.shape[0]
    cos32 = jnp.concatenate([cos, cos], axis=1)
    sin32 = jnp.concatenate([-sin, sin], axis=1)
    c_m = jnp.concatenate([jnp.ones((n, HD), F32), cos32, jnp.ones((n, LANES - HD - ROPE_D), F32)], axis=1)
    s_m = jnp.concatenate([jnp.zeros((n, HD), F32), sin32, jnp.zeros((n, LANES - HD - ROPE_D), F32)], axis=1)
    c_d = jnp.tile(cos32, (1, LANES // ROPE_D))
    s_d = jnp.tile(sin32, (1, LANES // ROPE_D))
    return jnp.concatenate([c_m, s_m, c_d, s_d], axis=1)


def _pad_cols(a, w):
    return jnp.pad(a, [(0, 0)] * (a.ndim - 1) + [(0, w - a.shape[-1])])


def _gain_table(P):
    L = P["mla_cq_norm"].shape[0]
    ones32 = jnp.ones((L, ROPE_D), F32)
    zeros32 = jnp.zeros((L, ROPE_D), F32)
    rows = [None] * GV_ROWS
    rows[G_CQ] = P["mla_cq_norm"]
    rows[G_CKV] = P["mla_ckv_norm"]
    rows[G_Q1] = jnp.tile(jnp.concatenate([P["mla_qn_gain"], P["mla_qr_gain"], zeros32], axis=1), (1, HEADS))
    rows[G_Q2] = jnp.tile(jnp.concatenate([P["mla_kn_gain"], ones32, zeros32], axis=1), (1, HEADS))
    rows[G_KR] = P["mla_kr_gain"]
    rows[G_FQ] = jnp.tile(P["fox_q_gain"], (1, HEADS))
    rows[G_FK] = P["fox_k_gain"]
    rows[G_FB] = P["fox_f_bias"]
    rows[G_DQ] = jnp.tile(P["diff_q_gain"], (1, 2 * HEADS))
    rows[G_DK] = jnp.tile(P["diff_k_gain"], (1, 2))
    cnt = jnp.tile(jnp.asarray([[1.0 / HD, 1.0 / ROPE_D]], F32), (L, HEADS))
    rows[G_QCNT] = jnp.concatenate([cnt, jnp.ones((L, LANES - 2 * HEADS), F32)], axis=1)
    rows = [jnp.zeros((L, GV_W), F32) if r is None else _pad_cols(r.astype(F32), GV_W) for r in rows]
    return jnp.stack(rows, axis=1)


def _win_padded(w_in):
    parts, start = [], 0
    for s in IN_SIZES:
        parts.append(w_in[..., start:start + s])
        start += s
    cq, ckv, kpe, fq, fk, fv, fz, sq, sk, sv, dq, dk, dv = parts
    cols = [cq, ckv, _pad_cols(kpe, LANES), fq, fk, fv, _pad_cols(fz, LANES), sq, sk, sv, dq, dk, dv]
    return jnp.concatenate(cols, axis=-1).astype(BF16)


def kernel(x_prompt, x_sample, mem_prompt, cache_mla, cache_fox, cache_sb, cache_diff, cache_mem, page_table, ffn1_norm, ffn1_w1, ffn1_w3, ffn1_w2, mix_norm, w_in, fox_f_bias, mla_cq_norm, mla_ckv_norm, mla_w_uq, mla_w_uk, mla_w_uv, mla_qn_gain, mla_qr_gain, mla_kn_gain, mla_kr_gain, fox_q_gain, fox_k_gain, diff_q_gain, diff_k_gain, diff_lam_q1, diff_lam_k1, diff_lam_q2, diff_lam_k2, diff_sub_gain, w_out, mem_q_norm, mem_kv_norm, mem_w_q, mem_w_k, mem_w_v, mem_q_gain, mem_k_gain, mem_w_o, ffn2_norm, ffn2_w1, ffn2_w3, ffn2_w2):
    batch, seq, d = x_prompt.shape
    ns = x_sample.shape[0]
    depth = w_in.shape[0]
    n_mem = mem_prompt.shape[1]
    n_pages = page_table.shape[1]
    page = cache_mla.shape[2]
    past = n_pages * page
    assert x_sample.shape[1] == 1
    mp = batch * seq

    tm_ffn = min(512, seq)
    tf = 512
    tm = min(256, seq)
    tq = min(512, seq)
    ppc = min(32, n_pages)
    kb = min(512, ppc * page)
    gsz = 8

    bf = lambda a: a.astype(BF16)
    f1 = (bf(ffn1_w1), bf(ffn1_w3), bf(ffn1_w2))
    f2 = (bf(ffn2_w1), bf(ffn2_w3), bf(ffn2_w2))
    win_p = _win_padded(w_in)
    wuq_p = bf(_pad_cols(mla_w_uq.reshape(depth, MLA_Q_RANK, HEADS, HD + ROPE_D), LANES)
               .reshape(depth, MLA_Q_RANK, HEADS * LANES))
    wuk_p = bf(_pad_cols(mla_w_uk, LANES).reshape(depth, MLA_KV_RANK, HEADS * LANES))
    wukt = bf(jnp.transpose(mla_w_uk.reshape(depth, MLA_KV_RANK, GROUP), (0, 2, 1)))
    wuv = bf(mla_w_uv.reshape(depth, MLA_KV_RANK, GROUP))
    wout = bf(w_out)
    wq, wk, wv, wo = bf(mem_w_q), bf(mem_w_k), bf(mem_w_v), bf(mem_w_o)
    P = dict(mla_cq_norm=mla_cq_norm, mla_ckv_norm=mla_ckv_norm, mla_qn_gain=mla_qn_gain, mla_qr_gain=mla_qr_gain,
             mla_kn_gain=mla_kn_gain, mla_kr_gain=mla_kr_gain, fox_q_gain=fox_q_gain, fox_k_gain=fox_k_gain,
             fox_f_bias=fox_f_bias, diff_q_gain=diff_q_gain, diff_k_gain=diff_k_gain)
    gv = _gain_table(P)
    emats = _group_matrices()
    lamv = _pad_cols(jnp.stack([diff_lam_q1, diff_lam_k1, diff_lam_q2, diff_lam_k2], axis=1).astype(F32), LANES)
    dg_pair = jnp.tile(diff_sub_gain.astype(F32), (1, 2))[:, None, :]
    dg_hi = jnp.concatenate([jnp.zeros((depth, 1, HD), F32), diff_sub_gain.astype(F32)[:, None, :]], axis=2)
    row1 = lambda a, l: a[l][None, :].astype(F32)

    tab_p = _rope_tables(jnp.tile(jnp.arange(seq, dtype=jnp.int32), batch))
    tab_s = _rope_tables(jnp.full((ns,), past, jnp.int32))
    caches = (jnp.transpose(cache_mla, (0, 1, 3, 2)), jnp.transpose(cache_fox, (0, 1, 3, 2)), cache_sb, cache_diff)

    xp = x_prompt.reshape(mp, d)
    xs = x_sample.reshape(ns, d)
    memf = mem_prompt.reshape(batch * n_mem, d)
    outs_p = [[] for _ in range(5)]
    outs_s = [[] for _ in range(4)]
    for l in range(depth):
        lam_init = 0.8 - 0.6 * math.exp(-0.3 * l)
        mkv = _memkv(memf, row1(mem_kv_norm, l), wk[l], wv[l], row1(mem_k_gain, l), min(256, n_mem))
        xp = _ffn(xp, row1(ffn1_norm, l), *f1, l, tm_ffn, tf)
        qmla, qfsd, kmla, ksh, r_mla, r_fox, r_sb, r_dif = _proj(
            xp, row1(mix_norm, l), win_p[l], wuq_p[l], wuk_p[l], gv[l], tab_p, emats, tm)
        logf = r_fox[:, 2 * HD:2 * HD + HEADS].reshape(batch, seq, HEADS)
        fcc, fcr = _fc(_pad_cols(logf, LANES), jnp.transpose(logf, (0, 2, 1)), tq)
        o4 = _attn_prompt(qmla, qfsd, kmla, ksh, fcc, fcr, wuv[l], dg_pair[l], lamv[l], batch, seq, tq, lam_init)
        x1, qm = _out_memq(xp, o4, wout[l], row1(mem_q_norm, l), wq[l], row1(mem_q_gain, l), tm)
        x2 = _mem_prompt(x1, qm, mkv, wo[l], batch, seq, n_mem, tm)
        xp = _ffn(x2, row1(ffn2_norm, l), *f2, l, tm_ffn, tf)
        for lst, r in zip(outs_p, (r_mla, r_fox, r_sb, r_dif, mkv)):
            lst.append(r)
        xs = _ffn(xs, row1(ffn1_norm, l), *f1, l, ns, tf)
        qmla, qfsd, kmla, ksh, r_mla, r_fox, r_sb, r_dif = _proj(
            xs, row1(mix_norm, l), win_p[l], wuq_p[l], wuk_p[l], gv[l], tab_s, emats, ns)
        q8m = qmla.reshape(ns, HEADS, LANES)
        fq8 = qfsd[:, 0:GROUP].reshape(ns, HEADS, HD)
        sq8 = qfsd[:, GROUP:2 * GROUP].reshape(ns, HEADS, HD)
        dq8 = qfsd[:, 2 * GROUP:3 * GROUP].reshape(ns, HEADS, HD)
        oa, ob, oc, od = _attn_sample(
            page_table, q8m, fq8, sq8, dq8, kmla.reshape(ns, HEADS, LANES), ksh.reshape(ns, 1, KSH_W),
            r_fox[:, 2 * HD:2 * HD + HEADS].reshape(ns, 1, HEADS), wukt[l], wuv[l], dg_hi[l], lamv[l],
            caches, l, ppc, kb, lam_init)
        hi = lambda a: a[:, :, HD:].reshape(ns, GROUP).astype(BF16)
        o4 = (oa.reshape(ns, GROUP).astype(BF16), ob.reshape(ns, GROUP).astype(BF16), hi(oc), hi(od))
        x1, qm = _out_memq(xs, o4, wout[l], row1(mem_q_norm, l), wq[l], row1(mem_q_gain, l), ns)
        x2 = _mem_sample(x1, qm.astype(F32).reshape(ns, MEM_HEADS, MEM_HD), cache_mem, l, wo[l], gsz)
        xs = _ffn(x2, row1(ffn2_norm, l), *f2, l, ns, tf)
        for lst, r in zip(outs_s, (r_mla, r_fox, r_sb, r_dif)):
            lst.append(r)

    stack_p = lambda lst: jnp.stack(lst).reshape(depth, batch, seq, -1)
    stack_s = lambda lst: jnp.stack(lst).reshape(depth, ns, 1, -1)
    return (xp.reshape(batch, seq, d), xs.reshape(ns, 1, d),
            stack_p(outs_p[0]), stack_p(outs_p[1]), stack_p(outs_p[2]), stack_p(outs_p[3]),
            jnp.stack(outs_p[4]).reshape(depth, batch, n_mem, 2, MEM_HEADS, MEM_HD),
            stack_s(outs_s[0]), stack_s(outs_s[1]), stack_s(outs_s[2]), stack_s(outs_s[3]))
```
